```python
import jax, jax.numpy as jnp
from jax import lax
import numpy as np

D_MODEL = 1024
BATCH = 32
SEQ = 2048
DEPTH = 4

CTX_LEN = 256
GRID_W = 64
SSD_EXPAND = 2
D_INNER = SSD_EXPAND * D_MODEL
SSD_HEAD_DIM = 64
SSD_HEADS = D_INNER // SSD_HEAD_DIM
SSD_GROUPS = 4
SSD_STATE = 128
SSD_CONV = 4
SSD_CHUNK = 128
SSD_GN = SSD_GROUPS * SSD_STATE
CONV_DIM = D_INNER + 2 * SSD_GN
SSD_PAD = ((SSD_CONV - 1) // 2, SSD_CONV // 2)
SC_WIDTH = D_MODEL
SC_CONV = 3
SC_PAD = ((SC_CONV - 1) // 2, SC_CONV // 2)
IN_WIDTHS = (D_INNER, CONV_DIM, 2 * SSD_HEADS, SC_WIDTH, SC_WIDTH, SC_WIDTH, D_MODEL, D_MODEL)
IN_COLS = sum(IN_WIDTHS)
N_EXPERTS = 16
N_EXPERT_GROUPS = 4
EXPERTS_PER_GROUP = N_EXPERTS // N_EXPERT_GROUPS
TOP_K = 2
D_EXPERT = 512
EPS = 1e-6

kernel_name = "hybrid_ssd_shortconv_moe_diffusion_trunk"


def rmsnorm(x, gain):
    xf = x.astype(jnp.float32)
    y = xf * lax.rsqrt(jnp.mean(xf * xf, axis=-1, keepdims=True) + EPS) * gain.astype(jnp.float32)
    return y.astype(x.dtype)


def modulate(h, gain, shift, scale):
    return rmsnorm(h, gain) * (1 + scale) + shift


def depthwise_conv(x, w, pad):
    return lax.conv_general_dilated(
        x, w[:, None, :].astype(x.dtype), window_strides=(1,), padding=[pad],
        dimension_numbers=("NWC", "WIO", "NWC"), feature_group_count=x.shape[-1])


def split_cols(p):
    idx = np.cumsum(np.array(IN_WIDTHS))[:-1].tolist()
    return jnp.split(p, idx, axis=-1)


def ssd_scan(x, dt, A, Bm, Cm, state0):
    b, L, H, P = x.shape
    G, N = Bm.shape[-2:]
    J = H // G
    nc = L // SSD_CHUNK
    f32 = jnp.float32

    def to_chunks(t):
        return jnp.moveaxis(t.reshape(b, nc, SSD_CHUNK, *t.shape[2:]), 1, 0)

    xs = to_chunks(x.astype(f32).reshape(b, L, G, J, P))
    dts = to_chunks(dt.astype(f32).reshape(b, L, G, J))
    Bs = to_chunks(Bm.astype(f32))
    Cs = to_chunks(Cm.astype(f32))
    Ag = A.astype(f32).reshape(G, J)
    lower = jnp.tril(jnp.ones((SSD_CHUNK, SSD_CHUNK), dtype=bool))[None, :, :, None, None]

    def step(state, inp):
        xc, dtc, Bc, Cc = inp
        a = jnp.cumsum(dtc * Ag, axis=1)
        seg = a[:, :, None] - a[:, None, :]
        decay = jnp.exp(jnp.where(lower, seg, -jnp.inf))
        cb = jnp.einsum("blgn,bsgn->blsg", Cc, Bc)
        xdt = xc * dtc[..., None]
        y_intra = jnp.einsum("blsg,blsgj,bsgjp->blgjp", cb, decay, xdt)
        y_inter = jnp.einsum("blgn,bgjpn->blgjp", Cc, state) * jnp.exp(a)[..., None]
        a_last = a[:, -1]
        w_in = jnp.exp(a_last[:, None] - a)[..., None]
        new_state = (state * jnp.exp(a_last)[..., None, None]
                     + jnp.einsum("bsgn,bsgjp->bgjpn", Bc, xdt * w_in))
        return new_state, y_intra + y_inter

    state, ys = lax.scan(step, state0.astype(f32).reshape(b, G, J, P, N), (xs, dts, Bs, Cs))
    y = jnp.moveaxis(ys, 0, 1).reshape(b, L, H, P)
    return y.astype(x.dtype), state.reshape(b, H, P, N)


def ssd_prep(xbc, dt_raw, conv_w, conv_b, dt_bias):
    b, L, _ = xbc.shape
    xbc = jax.nn.silu(depthwise_conv(xbc, conv_w, SSD_PAD) + conv_b)
    xs, bs, cs = jnp.split(xbc, [D_INNER, D_INNER + SSD_GN], axis=-1)
    dt = jax.nn.softplus(dt_raw.reshape(b, L, 2, SSD_HEADS).astype(jnp.float32)
                         + dt_bias.astype(jnp.float32))
    return (xs.reshape(b, L, SSD_HEADS, SSD_HEAD_DIM),
            bs.reshape(b, L, SSD_GROUPS, SSD_STATE),
            cs.reshape(b, L, SSD_GROUPS, SSD_STATE), dt)


def ssd_finish(y, xh, z, d_skip, gain, w_out):
    b, L = z.shape[:2]
    y = (y + xh * d_skip[:, None].astype(xh.dtype)).reshape(b, L, D_INNER)
    return rmsnorm(y * jax.nn.silu(z), gain) @ w_out


def short_conv(gb, gc, hv, conv_w, w_out, on_grid):
    v = gc * hv
    if on_grid:
        b, L, C = v.shape
        rows = L // GRID_W
        v = depthwise_conv(v.reshape(b * rows, GRID_W, C), conv_w, SC_PAD).reshape(b, L, C)
    else:
        v = depthwise_conv(v, conv_w, SC_PAD)
    return (gb * v) @ w_out


def merge(ga, gb, ya, yb, w_o):
    return (jax.nn.sigmoid(ga) * ya + jax.nn.sigmoid(gb) * yb) @ w_o


def hybrid_mixer(u, uc, w_in, conv_w, conv_b, dt_bias, a_log, d_skip, ssd_gain, w_ssd_out,
                 sc_w, w_sc_out, w_o, need_ctx):
    pl = split_cols(u @ w_in)
    pc = split_cols(uc @ w_in)
    xl, Bl, Cl, dtl = ssd_prep(pl[1], pl[2], conv_w, conv_b, dt_bias)
    xc, Bc, Cc, dtc = ssd_prep(pc[1], pc[2], conv_w, conv_b, dt_bias)
    A = -jnp.exp(a_log.astype(jnp.float32))
    zero = jnp.zeros((uc.shape[0], SSD_HEADS, SSD_HEAD_DIM, SSD_STATE), jnp.float32)

    def flip(t):
        return jnp.flip(t, axis=1)

    yc_f, s_f = ssd_scan(xc, dtc[:, :, 0], A[0], Bc, Cc, zero)
    yc_b, s_b = ssd_scan(flip(xc), flip(dtc[:, :, 1]), A[1], flip(Bc), flip(Cc), zero)
    yl_f, _ = ssd_scan(xl, dtl[:, :, 0], A[0], Bl, Cl, s_f)
    yl_b, _ = ssd_scan(flip(xl), flip(dtl[:, :, 1]), A[1], flip(Bl), flip(Cl), s_b)
    ssd_l = ssd_finish(yl_f + flip(yl_b), xl, pl[0], d_skip, ssd_gain, w_ssd_out)
    sc_l = short_conv(pl[3], pl[4], pl[5], sc_w, w_sc_out, True)
    out_l = merge(pl[6], pl[7], ssd_l, sc_l, w_o)
    if not need_ctx:
        return out_l, None
    ssd_c = ssd_finish(yc_f + flip(yc_b), xc, pc[0], d_skip, ssd_gain, w_ssd_out)
    sc_c = short_conv(pc[3], pc[4], pc[5], sc_w, w_sc_out, False)
    out_c = merge(pc[6], pc[7], ssd_c, sc_c, w_o)
    return out_l, out_c


def moe_ffn(u, w_router, b_router, w_gate, w_up, w_down):
    shp = u.shape
    t = u.reshape(-1, shp[-1])
    scores = jax.nn.sigmoid((t @ w_router).astype(jnp.float32))
    sel = scores + b_router.astype(jnp.float32)
    grouped = sel.reshape(-1, N_EXPERT_GROUPS, EXPERTS_PER_GROUP)
    group_score = lax.top_k(grouped, TOP_K)[0].sum(-1)
    best_group = jnp.argmax(group_score, axis=-1)
    in_group = (jnp.arange(N_EXPERTS) // EXPERTS_PER_GROUP)[None, :] == best_group[:, None]
    _, idx = lax.top_k(jnp.where(in_group, sel, -jnp.inf), TOP_K)
    w = jnp.take_along_axis(scores, idx, axis=-1)
    w = w / jnp.sum(w, axis=-1, keepdims=True)
    combine = jnp.einsum("tk,tke->te", w, jax.nn.one_hot(idx, N_EXPERTS, dtype=jnp.float32)).astype(u.dtype)
    out = jnp.zeros_like(t)
    for e in range(N_EXPERTS):
        hid = jax.nn.silu(t @ w_gate[e]) * (t @ w_up[e])
        out = out + combine[:, e:e + 1] * (hid @ w_down[e])
    return out.reshape(shp)


def setup_inputs(seed: int = 0) -> dict:
    key = jax.random.key(seed)
    ks = jax.random.split(key, 26)
    f32 = jnp.float32

    def nrm(k, shape, s):
        return jax.random.normal(k, shape, f32) * s

    dt0 = jnp.exp(jax.random.uniform(ks[11], (DEPTH, 2, SSD_HEADS), f32)
                  * (np.log(0.1) - np.log(0.001)) + np.log(0.001))
    return {
        "x": nrm(ks[0], (BATCH, SEQ, D_MODEL), 1.0),
        "c": nrm(ks[1], (BATCH, D_MODEL), 1.0),
        "ctx": nrm(ks[2], (BATCH, CTX_LEN, D_MODEL), 1.0),
        "c_ctx": nrm(ks[3], (D_MODEL,), 1.0),
        "w_ada": nrm(ks[4], (DEPTH, D_MODEL, 6 * D_MODEL), 0.5 * D_MODEL ** -0.5),
        "b_ada": nrm(ks[5], (DEPTH, 6 * D_MODEL), 0.01),
        "norm_mix": 1.0 + nrm(ks[6], (DEPTH, D_MODEL), 0.02),
        "norm_ffn": 1.0 + nrm(ks[7], (DEPTH, D_MODEL), 0.02),
        "w_in": nrm(ks[8], (DEPTH, D_MODEL, IN_COLS), D_MODEL ** -0.5),
        "ssd_conv_w": nrm(ks[9], (DEPTH, SSD_CONV, CONV_DIM), SSD_CONV ** -0.5),
        "ssd_conv_b": nrm(ks[10], (DEPTH, CONV_DIM), 0.01),
        "ssd_dt_bias": dt0 + jnp.log(-jnp.expm1(-dt0)),
        "ssd_a_log": jnp.log(jax.random.uniform(ks[12], (DEPTH, 2, SSD_HEADS), f32, 1.0, 16.0)),
        "ssd_d": 1.0 + nrm(ks[13], (DEPTH, SSD_HEADS), 0.1),
        "ssd_norm": 1.0 + nrm(ks[14], (DEPTH, D_INNER), 0.02),
        "w_ssd_out": nrm(ks[15], (DEPTH, D_INNER, D_MODEL), D_INNER ** -0.5),
        "sc_conv_w": nrm(ks[16], (DEPTH, SC_CONV, SC_WIDTH), SC_CONV ** -0.5),
        "w_sc_out": nrm(ks[17], (DEPTH, SC_WIDTH, D_MODEL), SC_WIDTH ** -0.5),
        "w_o": nrm(ks[18], (DEPTH, D_MODEL, D_MODEL), D_MODEL ** -0.5),
        "w_router": nrm(ks[19], (D_MODEL, N_EXPERTS), D_MODEL ** -0.5),
        "b_router": nrm(ks[20], (N_EXPERTS,), 0.01),
        "w_gate": nrm(ks[21], (DEPTH, N_EXPERTS, D_MODEL, D_EXPERT), D_MODEL ** -0.5),
        "w_up": nrm(ks[22], (DEPTH, N_EXPERTS, D_MODEL, D_EXPERT), D_MODEL ** -0.5),
        "w_down": nrm(ks[23], (DEPTH, N_EXPERTS, D_EXPERT, D_MODEL), D_EXPERT ** -0.5),
        "final_norm": 1.0 + nrm(ks[24], (D_MODEL,), 0.02),
    }


def reference(x, c, ctx, c_ctx, w_ada, b_ada, norm_mix, norm_ffn, w_in, ssd_conv_w, ssd_conv_b,
              ssd_dt_bias, ssd_a_log, ssd_d, ssd_norm, w_ssd_out, sc_conv_w, w_sc_out, w_o,
              w_router, b_router, w_gate, w_up, w_down, final_norm):
    h, hc = x, ctx
    n_ctx = ctx.shape[1]
    cond = jax.nn.silu(c)
    cond_ctx = jax.nn.silu(c_ctx)
    for l in range(DEPTH):
        last = l == DEPTH - 1
        m = jnp.split((cond @ w_ada[l] + b_ada[l])[:, None, :], 6, axis=-1)
        mc = jnp.split(cond_ctx @ w_ada[l] + b_ada[l], 6, axis=-1)
        u = modulate(h, norm_mix[l], m[0], m[1])
        uc = modulate(hc, norm_mix[l], mc[0], mc[1])
        y, yc = hybrid_mixer(u, uc, w_in[l], ssd_conv_w[l], ssd_conv_b[l], ssd_dt_bias[l], ssd_a_log[l],
                             ssd_d[l], ssd_norm[l], w_ssd_out[l], sc_conv_w[l], w_sc_out[l], w_o[l],
                             not last)
        h = h + m[2] * y
        if last:
            u2 = modulate(h, norm_ffn[l], m[3], m[4])
            h = h + m[5] * moe_ffn(u2, w_router, b_router, w_gate[l], w_up[l], w_down[l])
        else:
            hc = hc + mc[2] * yc
            u2 = jnp.concatenate([modulate(hc, norm_ffn[l], mc[3], mc[4]),
                                  modulate(h, norm_ffn[l], m[3], m[4])], axis=1)
            f = moe_ffn(u2, w_router, b_router, w_gate[l], w_up[l], w_down[l])
            hc = hc + mc[5] * f[:, :n_ctx]
            h = h + m[5] * f[:, n_ctx:]
    return rmsnorm(h, final_norm)
```

```python
import functools

import jax
import jax.numpy as jnp
from jax import lax
from jax.experimental import pallas as pl
from jax.experimental.pallas import tpu as pltpu

F32 = jnp.float32
BF16 = jnp.bfloat16
HIGHEST = lax.Precision.HIGHEST

D_MODEL = 1024
D_INNER = 2048
HEAD_DIM = 64
HEADS = 32
GROUPS = 4
HEADS_PER_GROUP = HEADS // GROUPS
GROUP_WIDTH = HEADS_PER_GROUP * HEAD_DIM
STATE = 128
CHUNK = 128
CONV_DIM = D_INNER + 2 * GROUPS * STATE
SSD_CONV = 4
SC_CONV = 3
GRID_W = 64
N_EXPERTS = 16
EXPERTS_PER_GROUP = 4
D_EXPERT = 512
EPS = 1e-6
P_COLS = 10 * D_MODEL
DT_PAD = 128
ROW_TILE = 256
VMEM_LIMIT = 56 * 1024 * 1024


def _params(*sem):
    return pltpu.CompilerParams(dimension_semantics=sem, vmem_limit_bytes=VMEM_LIMIT)


def _silu(v):
    return v * jax.nn.sigmoid(v)


def _modulated_norm(x, gain, shift, scale):
    ms = jnp.mean(x * x, axis=-1, keepdims=True)
    return x * lax.rsqrt(ms + EPS) * gain * (1.0 + scale) + shift


def _ada_kernel(c_ref, w_ref, b_ref, o_ref):
    cond = _silu(c_ref[...])
    o_ref[...] = jnp.dot(cond, w_ref[...], precision=HIGHEST, preferred_element_type=F32) + b_ref[...]


def _ada_params(cc, w_ada, b_ada):
    depth, _, n = w_ada.shape
    rows = cc.shape[0]
    tn = 512
    return pl.pallas_call(
        _ada_kernel,
        grid=(depth, n // tn),
        in_specs=[pl.BlockSpec((rows, D_MODEL), lambda l, j: (0, 0)),
                  pl.BlockSpec((None, D_MODEL, tn), lambda l, j: (l, 0, j)),
                  pl.BlockSpec((None, 1, tn), lambda l, j: (l, 0, j))],
        out_specs=pl.BlockSpec((None, rows, tn), lambda l, j: (l, 0, j)),
        out_shape=jax.ShapeDtypeStruct((depth, rows, n), F32),
        compiler_params=_params("parallel", "parallel"),
        name="ada_params",
    )(cc, w_ada, b_ada.reshape(depth, 1, n))


def _in_proj_kernel(h_ref, mod_ref, gain_ref, w_ref, wdt_ref, dtb_ref, p_ref, dt_ref):
    u = _modulated_norm(h_ref[...], gain_ref[...], mod_ref[0:1, :], mod_ref[1:2, :]).astype(BF16)
    for n0 in range(0, P_COLS, 512):
        p_ref[:, n0:n0 + 512] = jnp.dot(u, w_ref[:, n0:n0 + 512], preferred_element_type=F32).astype(BF16)
    raw = jnp.dot(u, wdt_ref[...], preferred_element_type=F32) + dtb_ref[...]
    dt_ref[...] = jnp.maximum(raw, 0.0) + jnp.log(1.0 + jnp.exp(-jnp.abs(raw)))


def _in_proj(h, mods, gain, w_main, w_dt, dt_bias, mod_index):
    t = h.shape[0]
    return pl.pallas_call(
        _in_proj_kernel,
        grid=(t // ROW_TILE,),
        in_specs=[pl.BlockSpec((ROW_TILE, D_MODEL), lambda i: (i, 0)),
                  pl.BlockSpec((None, None, 6, D_MODEL), mod_index),
                  pl.BlockSpec((1, D_MODEL), lambda i: (0, 0)),
                  pl.BlockSpec((D_MODEL, P_COLS), lambda i: (0, 0), pipeline_mode=pl.Buffered(1)),
                  pl.BlockSpec((D_MODEL, DT_PAD), lambda i: (0, 0)),
                  pl.BlockSpec((1, DT_PAD), lambda i: (0, 0))],
        out_specs=[pl.BlockSpec((ROW_TILE, P_COLS), lambda i: (i, 0)),
                   pl.BlockSpec((ROW_TILE, DT_PAD), lambda i: (i, 0))],
        out_shape=[jax.ShapeDtypeStruct((t, P_COLS), BF16),
                   jax.ShapeDtypeStruct((t, DT_PAD), F32)],
        compiler_params=_params("parallel"),
        name="in_proj",
    )(h, mods, gain, w_main, w_dt, dt_bias)


def _ssd_conv_kernel(x_ref, w_ref, b_ref, o_ref, *, n_ctx):
    x = x_ref[...].astype(F32)
    s_len = x.shape[0]
    rows = lax.broadcasted_iota(jnp.int32, (s_len, 1), 0)
    acc = x * w_ref[1:2, :]
    for k in (0, 2, 3):
        off = k - 1
        src = rows + off
        ok = (src >= 0) & (src < s_len) & ((src >= n_ctx) == (rows >= n_ctx))
        shifted = pltpu.roll(x, (-off) % s_len, 0)
        acc = acc + jnp.where(ok, shifted, 0.0) * w_ref[k:k + 1, :]
    o_ref[...] = _silu(acc + b_ref[...]).astype(BF16)


def _ssd_conv(p3, conv_w, conv_b, n_ctx):
    b, s_len, _ = p3.shape
    tc = 256
    first = D_INNER // tc
    return pl.pallas_call(
        functools.partial(_ssd_conv_kernel, n_ctx=n_ctx),
        grid=(b, CONV_DIM // tc),
        in_specs=[pl.BlockSpec((None, s_len, tc), lambda i, c: (i, 0, first + c)),
                  pl.BlockSpec((SSD_CONV, tc), lambda i, c: (0, c)),
                  pl.BlockSpec((1, tc), lambda i, c: (0, c))],
        out_specs=pl.BlockSpec((None, s_len, tc), lambda i, c: (i, 0, c)),
        out_shape=jax.ShapeDtypeStruct((b, s_len, CONV_DIM), BF16),
        compiler_params=_params("parallel", "parallel"),
        name="ssd_conv",
    )(p3, conv_w, conv_b)


def _decay_kernel(dt_ref, alog_ref, a_ref, ea_ref, dw_ref, eal_ref):
    dt = dt_ref[...]
    da = dt * (-jnp.exp(alog_ref[...]))
    r = lax.broadcasted_iota(jnp.int32, (CHUNK, CHUNK), 0)
    c = lax.broadcasted_iota(jnp.int32, (CHUNK, CHUNK), 1)
    fwd = jnp.dot((c <= r).astype(F32), da, precision=HIGHEST, preferred_element_type=F32)
    bwd = jnp.dot((c >= r).astype(F32), da, precision=HIGHEST, preferred_element_type=F32)
    is_fwd = lax.broadcasted_iota(jnp.int32, (1, DT_PAD), 1) < HEADS
    a = jnp.where(is_fwd, fwd, bwd)
    total = jnp.where(is_fwd, fwd[CHUNK - 1:CHUNK, :], bwd[0:1, :])
    a_ref[...] = a
    ea_ref[...] = jnp.exp(a)
    dw_ref[...] = dt * jnp.exp(total - a)
    eal_ref[...] = jnp.exp(total)


def _decay_tables(dt3, alog):
    b, s_len, _ = dt3.shape
    nc = s_len // CHUNK
    tile = pl.BlockSpec((None, CHUNK, DT_PAD), lambda i, c: (i, c, 0))
    full = jax.ShapeDtypeStruct((b, s_len, DT_PAD), F32)
    return pl.pallas_call(
        _decay_kernel,
        grid=(b, nc),
        in_specs=[tile, pl.BlockSpec((1, DT_PAD), lambda i, c: (0, 0))],
        out_specs=[tile, tile, tile, pl.BlockSpec((None, None, 1, DT_PAD), lambda i, c: (i, c, 0, 0))],
        out_shape=[full, full, full, jax.ShapeDtypeStruct((b, nc, 1, DT_PAD), F32)],
        compiler_params=_params("parallel", "parallel"),
        name="ssd_decay",
    )(dt3, alog)


def _scan_kernel(x_ref, b_ref, c_ref, col_ref, row_ref, eal_ref, exp_ref, y_ref,
                 state_f, state_b, yacc, *, n_chunks, n_ctx_chunks):
    yacc[...] = jnp.zeros_like(yacc)
    state_f[...] = jnp.zeros_like(state_f)
    state_b[...] = jnp.zeros_like(state_b)
    r = lax.broadcasted_iota(jnp.int32, (CHUNK, CHUNK), 0)
    c = lax.broadcasted_iota(jnp.int32, (CHUNK, CHUNK), 1)
    masks = (r >= c, r <= c)
    low_half = lax.broadcasted_iota(jnp.int32, (1, 2 * HEAD_DIM), 1) < HEAD_DIM
    states = (state_f, state_b)

    def chunk_step(ci, d):
        r0 = pl.multiple_of(ci * CHUNK, CHUNK)
        xc = x_ref[pl.ds(r0, CHUNK), :]
        bc = b_ref[pl.ds(r0, CHUNK), :]
        cc = c_ref[pl.ds(r0, CHUNK), :]
        cols = col_ref[pl.ds(r0, CHUNK), :]
        arow = row_ref[d, 0, ci]
        dtrow = row_ref[d, 1, ci]
        cb = lax.dot_general(cc, bc, (((1,), (1,)), ((), ())), preferred_element_type=F32)
        spread = jnp.dot(cols.astype(BF16), exp_ref[d], preferred_element_type=F32)
        ea_x = spread[:, :GROUP_WIDTH]
        dw_x = spread[:, GROUP_WIDTH:]
        st = states[d][...]
        y = jnp.dot(cc, st.astype(BF16), preferred_element_type=F32) * ea_x
        parts = []
        for k in range(HEADS_PER_GROUP // 2):
            lhs = []
            for j in (2 * k, 2 * k + 1):
                col = 24 * d + j
                seg = cols[:, col:col + 1] - arow[j:j + 1, :]
                dec = jnp.exp(jnp.where(masks[d], seg, -jnp.inf))
                lhs.append((cb * dec * dtrow[j:j + 1, :]).astype(BF16))
            xp = xc[:, 128 * k:128 * (k + 1)]
            zero = jnp.zeros_like(xp)
            rhs = jnp.concatenate([jnp.where(low_half, xp, zero), jnp.where(low_half, zero, xp)], axis=0)
            parts.append(jnp.dot(jnp.concatenate(lhs, axis=1), rhs, preferred_element_type=F32))
        y = y + jnp.concatenate(parts, axis=1)
        yacc[pl.ds(r0, CHUNK), :] += y
        xw = (xc.astype(F32) * dw_x).astype(BF16)
        upd = lax.dot_general(bc, xw, (((0,), (0,)), ((), ())), preferred_element_type=F32)
        states[d][...] = st * eal_ref[ci, pl.ds(d, 1), :] + upd

    def body(i, carry):
        chunk_step(i, 0)
        cb_idx = jnp.where(i < n_ctx_chunks, n_ctx_chunks - 1 - i, n_chunks + n_ctx_chunks - 1 - i)
        chunk_step(cb_idx, 1)
        return carry

    lax.fori_loop(0, n_chunks, body, 0)
    y_ref[...] = yacc[...].astype(BF16)


def _ssd_scan(xbc, colslab, rowslab, ealx, spread_mat, n_ctx):
    b, s_len, _ = xbc.shape
    nc = s_len // CHUNK
    x_blocks = D_INNER // STATE
    return pl.pallas_call(
        functools.partial(_scan_kernel, n_chunks=nc, n_ctx_chunks=n_ctx // CHUNK),
        grid=(b, GROUPS),
        in_specs=[pl.BlockSpec((None, s_len, GROUP_WIDTH), lambda i, g: (i, 0, g)),
                  pl.BlockSpec((None, s_len, STATE), lambda i, g: (i, 0, x_blocks + g)),
                  pl.BlockSpec((None, s_len, STATE), lambda i, g: (i, 0, x_blocks + GROUPS + g)),
                  pl.BlockSpec((None, None, s_len, 48), lambda i, g: (i, g, 0, 0)),
                  pl.BlockSpec((None, None, 2, 2, nc, HEADS_PER_GROUP, CHUNK), lambda i, g: (i, g, 0, 0, 0, 0, 0)),
                  pl.BlockSpec((None, None, nc, 2, GROUP_WIDTH), lambda i, g: (i, g, 0, 0, 0)),
                  pl.BlockSpec((2, 48, 2 * GROUP_WIDTH), lambda i, g: (0, 0, 0))],
        out_specs=pl.BlockSpec((None, s_len, GROUP_WIDTH), lambda i, g: (i, 0, g)),
        out_shape=jax.ShapeDtypeStruct((b, s_len, D_INNER), BF16),
        scratch_shapes=[pltpu.VMEM((STATE, GROUP_WIDTH), F32),
                        pltpu.VMEM((STATE, GROUP_WIDTH), F32),
                        pltpu.VMEM((s_len, GROUP_WIDTH), F32)],
        compiler_params=_params("parallel", "parallel"),
        name="ssd_scan",
    )(xbc, xbc, xbc, colslab, rowslab, ealx, spread_mat)


def _spread_matrix():
    rows = jnp.arange(48)[:, None]
    lanes = jnp.arange(2 * GROUP_WIDTH)[None, :]
    mats = []
    for d in range(2):
        kind = (rows - 24 * d) // HEADS_PER_GROUP
        head = (rows - 24 * d) % HEADS_PER_GROUP
        valid = (rows >= 24 * d + 8) & (rows < 24 * d + 24)
        hit = valid & (lanes // GROUP_WIDTH == kind - 1) & ((lanes % GROUP_WIDTH) // HEAD_DIM == head)
        mats.append(hit.astype(BF16))
    return jnp.stack(mats)


def _mixer_out_kernel(h_ref, y_ref, x_ref, z_ref, scb_ref, scc_ref, sch_ref, ga_ref, gb_ref, mod_ref,
                      dskip_ref, gain_ref, wssd_ref, scw_ref, wsc_ref, wo_ref, o_ref, *, tiles_per_batch, n_ctx):
    yv = y_ref[...].astype(F32) + x_ref[...].astype(F32) * dskip_ref[...]
    yz = yv * _silu(z_ref[...].astype(F32))
    ms = jnp.mean(yz * yz, axis=-1, keepdims=True)
    yn = (yz * lax.rsqrt(ms + EPS) * gain_ref[...]).astype(BF16)
    ssd = jnp.dot(yn, wssd_ref[...], preferred_element_type=F32)

    v = scc_ref[...].astype(F32) * sch_ref[...].astype(F32)
    tm = v.shape[0]
    pos = (pl.program_id(0) % tiles_per_batch) * tm + lax.broadcasted_iota(jnp.int32, (tm, 1), 0)
    in_ctx = pos < n_ctx
    col = (pos - n_ctx) % GRID_W
    first = jnp.where(in_ctx, pos, col)
    last = jnp.where(in_ctx, n_ctx - 1 - pos, GRID_W - 1 - col)
    left_ok = first != 0
    right_ok = last != 0
    vl = jnp.where(left_ok, pltpu.roll(v, 1, 0), 0.0)
    vr = jnp.where(right_ok, pltpu.roll(v, tm - 1, 0), 0.0)
    cv = vl * scw_ref[0:1, :] + v * scw_ref[1:2, :] + vr * scw_ref[2:3, :]
    sc = jnp.dot((scb_ref[...].astype(F32) * cv).astype(BF16), wsc_ref[...], preferred_element_type=F32)

    mix = jax.nn.sigmoid(ga_ref[...].astype(F32)) * ssd + jax.nn.sigmoid(gb_ref[...].astype(F32)) * sc
    out = jnp.dot(mix.astype(BF16), wo_ref[...], preferred_element_type=F32)
    o_ref[...] = h_ref[...] + mod_ref[2:3, :] * out


def _mixer_out(h, y, xbc, p, mods, mod_index, d_skip, gain, w_ssd, sc_w, w_sc, w_o, tiles_per_batch, n_ctx):
    t = h.shape[0]
    tm = ROW_TILE
    row = lambda width, j: pl.BlockSpec((tm, width), lambda i: (i, j))
    const = lambda shape: pl.BlockSpec(shape, lambda i: (0,) * len(shape))
    return pl.pallas_call(
        functools.partial(_mixer_out_kernel, tiles_per_batch=tiles_per_batch, n_ctx=n_ctx),
        grid=(t // tm,),
        in_specs=[row(D_MODEL, 0), row(D_INNER, 0), row(D_INNER, 0), row(D_INNER, 0),
                  row(D_MODEL, 5), row(D_MODEL, 6), row(D_MODEL, 7), row(D_MODEL, 8), row(D_MODEL, 9),
                  pl.BlockSpec((None, None, 6, D_MODEL), mod_index),
                  const((1, D_INNER)), const((1, D_INNER)), const((D_INNER, D_MODEL)),
                  const((SC_CONV, D_MODEL)), const((D_MODEL, D_MODEL)), const((D_MODEL, D_MODEL))],
        out_specs=row(D_MODEL, 0),
        out_shape=jax.ShapeDtypeStruct((t, D_MODEL), F32),
        compiler_params=_params("parallel"),
        name="mixer_out",
    )(h, y, xbc, p, p, p, p, p, p, mods, d_skip, gain, w_ssd, sc_w, w_sc, w_o)


def _router_kernel(h_ref, mod_ref, gain_ref, wr_ref, br_ref, u_ref, comb_ref):
    u = _modulated_norm(h_ref[...], gain_ref[...], mod_ref[3:4, :], mod_ref[4:5, :])
    u_ref[...] = u.astype(BF16)
    logits = lax.dot_general(wr_ref[...], u, (((1,), (1,)), ((), ())), precision=HIGHEST,
                             preferred_element_type=F32)
    scores = jax.nn.sigmoid(logits)
    sel = scores + br_ref[...]
    best_val = None
    best_group = None
    for g in range(N_EXPERTS // EXPERTS_PER_GROUP):
        v = [sel[EXPERTS_PER_GROUP * g + i:EXPERTS_PER_GROUP * g + i + 1, :] for i in range(EXPERTS_PER_GROUP)]
        top2 = None
        for i in range(EXPERTS_PER_GROUP):
            for j in range(i + 1, EXPERTS_PER_GROUP):
                pair = v[i] + v[j]
                top2 = pair if top2 is None else jnp.maximum(top2, pair)
        if g == 0:
            best_val, best_group = top2, jnp.zeros_like(top2, dtype=jnp.int32)
        else:
            better = top2 > best_val
            best_group = jnp.where(better, g, best_group)
            best_val = jnp.where(better, top2, best_val)
    eidx = lax.broadcasted_iota(jnp.int32, sel.shape, 0)
    masked = jnp.where(eidx // EXPERTS_PER_GROUP == best_group, sel, -jnp.inf)
    m1 = jnp.max(masked, axis=0, keepdims=True)
    i1 = jnp.min(jnp.where(masked == m1, eidx, N_EXPERTS), axis=0, keepdims=True)
    masked2 = jnp.where(eidx == i1, -jnp.inf, masked)
    m2 = jnp.max(masked2, axis=0, keepdims=True)
    i2 = jnp.min(jnp.where(masked2 == m2, eidx, N_EXPERTS), axis=0, keepdims=True)
    w1 = jnp.sum(jnp.where(eidx == i1, scores, 0.0), axis=0, keepdims=True)
    w2 = jnp.sum(jnp.where(eidx == i2, scores, 0.0), axis=0, keepdims=True)
    denom = w1 + w2
    comb_ref[...] = jnp.where(eidx == i1, w1 / denom, 0.0) + jnp.where(eidx == i2, w2 / denom, 0.0)


def _router(h, mods, mod_index, gain, wr_t, b_router):
    t = h.shape[0]
    tm = ROW_TILE
    return pl.pallas_call(
        _router_kernel,
        grid=(t // tm,),
        in_specs=[pl.BlockSpec((tm, D_MODEL), lambda i: (i, 0)),
                  pl.BlockSpec((None, None, 6, D_MODEL), mod_index),
                  pl.BlockSpec((1, D_MODEL), lambda i: (0, 0)),
                  pl.BlockSpec((N_EXPERTS, D_MODEL), lambda i: (0, 0)),
                  pl.BlockSpec((N_EXPERTS, 1), lambda i: (0, 0))],
        out_specs=[pl.BlockSpec((tm, D_MODEL), lambda i: (i, 0)),
                   pl.BlockSpec((N_EXPERTS, tm), lambda i: (0, i))],
        out_shape=[jax.ShapeDtypeStruct((t, D_MODEL), BF16),
                   jax.ShapeDtypeStruct((N_EXPERTS, t), F32)],
        compiler_params=_params("parallel"),
        name="router",
    )(h, mods, gain, wr_t, b_router)


def _experts_kernel(u_ref, comb_ref, h_ref, mod_ref, wg_ref, wu_ref, wd_ref, o_ref, acc_ref):
    e = pl.program_id(1)

    @pl.when(e == 0)
    def _():
        acc_ref[...] = jnp.zeros_like(acc_ref)

    u = u_ref[...]
    hid = _silu(jnp.dot(u, wg_ref[...], preferred_element_type=F32)) * jnp.dot(u, wu_ref[...], preferred_element_type=F32)
    acc_ref[...] += comb_ref[...] * jnp.dot(hid.astype(BF16), wd_ref[...], preferred_element_type=F32)

    @pl.when(e == pl.num_programs(1) - 1)
    def _():
        o_ref[...] = h_ref[...] + mod_ref[5:6, :] * acc_ref[...]


def _experts(u, comb3, h, mods, mod_index, w_gate, w_up, w_down):
    t = h.shape[0]
    tm = ROW_TILE
    return pl.pallas_call(
        _experts_kernel,
        grid=(t // tm, N_EXPERTS),
        in_specs=[pl.BlockSpec((tm, D_MODEL), lambda i, e: (i, 0)),
                  pl.BlockSpec((None, tm, 1), lambda i, e: (e, i, 0)),
                  pl.BlockSpec((tm, D_MODEL), lambda i, e: (i, 0)),
                  pl.BlockSpec((None, None, 6, D_MODEL), lambda i, e: mod_index(i)),
                  pl.BlockSpec((None, D_MODEL, D_EXPERT), lambda i, e: (e, 0, 0)),
                  pl.BlockSpec((None, D_MODEL, D_EXPERT), lambda i, e: (e, 0, 0)),
                  pl.BlockSpec((None, D_EXPERT, D_MODEL), lambda i, e: (e, 0, 0))],
        out_specs=pl.BlockSpec((tm, D_MODEL), lambda i, e: (i, 0)),
        out_shape=jax.ShapeDtypeStruct((t, D_MODEL), F32),
        scratch_shapes=[pltpu.VMEM((tm, D_MODEL), F32)],
        compiler_params=_params("parallel", "arbitrary"),
        name="experts",
    )(u, comb3, h, mods, w_gate, w_up, w_down)


def _final_norm_kernel(h_ref, gain_ref, o_ref):
    x = h_ref[...]
    ms = jnp.mean(x * x, axis=-1, keepdims=True)
    o_ref[...] = x * lax.rsqrt(ms + EPS) * gain_ref[...]


def _final_norm(h3, gain, n_ctx):
    b, s_len, _ = h3.shape
    tm = ROW_TILE
    skip = n_ctx // tm
    return pl.pallas_call(
        _final_norm_kernel,
        grid=(b, (s_len - n_ctx) // tm),
        in_specs=[pl.BlockSpec((None, tm, D_MODEL), lambda i, j: (i, skip + j, 0)),
                  pl.BlockSpec((1, D_MODEL), lambda i, j: (0, 0))],
        out_specs=pl.BlockSpec((None, tm, D_MODEL), lambda i, j: (i, j, 0)),
        out_shape=jax.ShapeDtypeStruct((b, s_len - n_ctx, D_MODEL), F32),
        compiler_params=_params("parallel", "parallel"),
        name="final_norm",
    )(h3, gain)


def _group_major(t, b, s_len):
    return t[:, :, :2 * HEADS].reshape(b, s_len, 2, GROUPS, HEADS_PER_GROUP)


def kernel(x, c, ctx, c_ctx, w_ada, b_ada, norm_mix, norm_ffn, w_in, ssd_conv_w, ssd_conv_b, ssd_dt_bias,
           ssd_a_log, ssd_d, ssd_norm, w_ssd_out, sc_conv_w, w_sc_out, w_o, w_router, b_router, w_gate, w_up,
           w_down, final_norm):
    b, seq, _ = x.shape
    n_ctx = ctx.shape[1]
    depth = w_ada.shape[0]
    s_len = n_ctx + seq
    t = b * s_len
    nc = s_len // CHUNK
    assert n_ctx == ROW_TILE and seq % ROW_TILE == 0 and ROW_TILE % GRID_W == 0 and ROW_TILE % CHUNK == 0
    tiles_per_batch = s_len // ROW_TILE

    def mod_index(i):
        return (i // tiles_per_batch, jnp.minimum(i % tiles_per_batch, 1), 0, 0)

    rows = -(-(b + 1) // 8) * 8
    cc = jnp.zeros((rows, D_MODEL), F32).at[:b].set(c).at[b].set(c_ctx)
    ada = _ada_params(cc, w_ada, b_ada)
    lat = ada[:, :b].reshape(depth, b, 1, 6, D_MODEL)
    con = jnp.broadcast_to(ada[:, b].reshape(depth, 1, 1, 6, D_MODEL), lat.shape)
    mods_all = jnp.concatenate([con, lat], axis=2)

    spread_mat = _spread_matrix()
    wr_t = w_router.T
    br = b_router.reshape(N_EXPERTS, 1)
    h = jnp.concatenate([ctx, x], axis=1).reshape(t, D_MODEL)

    for l in range(depth):
        mods = mods_all[l]
        w_l = w_in[l]
        w_main = jnp.concatenate([w_l[:, :D_INNER + CONV_DIM], w_l[:, D_INNER + CONV_DIM + 2 * HEADS:]],
                                 axis=1).astype(BF16)
        w_dt = jnp.pad(w_l[:, D_INNER + CONV_DIM:D_INNER + CONV_DIM + 2 * HEADS],
                       ((0, 0), (0, DT_PAD - 2 * HEADS))).astype(BF16)
        dt_bias = jnp.pad(ssd_dt_bias[l].reshape(1, 2 * HEADS), ((0, 0), (0, DT_PAD - 2 * HEADS)))
        alog = jnp.pad(ssd_a_log[l].reshape(1, 2 * HEADS), ((0, 0), (0, DT_PAD - 2 * HEADS)))

        p, dt = _in_proj(h, mods, norm_mix[l].reshape(1, D_MODEL), w_main, w_dt, dt_bias, mod_index)
        xbc = _ssd_conv(p.reshape(b, s_len, P_COLS), ssd_conv_w[l], ssd_conv_b[l].reshape(1, CONV_DIM), n_ctx)
        dt3 = dt.reshape(b, s_len, DT_PAD)
        a, ea, dw, eal = _decay_tables(dt3, alog)

        cols = jnp.stack([_group_major(v, b, s_len) for v in (a, ea, dw)], axis=3)
        colslab = cols.transpose(0, 4, 1, 2, 3, 5).reshape(b, GROUPS, s_len, 48)
        rows_t = jnp.stack([_group_major(v, b, s_len) for v in (a, dt3)], axis=3)
        rowslab = rows_t.reshape(b, nc, CHUNK, 2, 2, GROUPS, HEADS_PER_GROUP).transpose(0, 5, 3, 4, 1, 6, 2)
        ealx = eal[:, :, 0, :2 * HEADS].reshape(b, nc, 2, GROUPS, HEADS_PER_GROUP).transpose(0, 3, 1, 2, 4)
        ealx = jnp.repeat(ealx, HEAD_DIM, axis=-1)

        y = _ssd_scan(xbc, colslab, rowslab, ealx, spread_mat, n_ctx)
        h = _mixer_out(h, y.reshape(t, D_INNER), xbc.reshape(t, CONV_DIM), p, mods, mod_index,
                       jnp.repeat(ssd_d[l], HEAD_DIM).reshape(1, D_INNER), ssd_norm[l].reshape(1, D_INNER),
                       w_ssd_out[l].astype(BF16), sc_conv_w[l], w_sc_out[l].astype(BF16), w_o[l].astype(BF16),
                       tiles_per_batch, n_ctx)
        u2, comb = _router(h, mods, mod_index, norm_ffn[l].reshape(1, D_MODEL), wr_t, br)
        h = _experts(u2, comb.reshape(N_EXPERTS, t, 1), h, mods, mod_index,
                     w_gate[l].astype(BF16), w_up[l].astype(BF16), w_down[l].astype(BF16))

    return _final_norm(h.reshape(b, s_len, D_MODEL), final_norm.reshape(1, D_MODEL), n_ctx)
```

```python
import functools

import jax
import jax.numpy as jnp
from jax import lax
from jax.experimental import pallas as pl
from jax.experimental.pallas import tpu as pltpu

F32 = jnp.float32
BF16 = jnp.bfloat16
HIGHEST = lax.Precision.HIGHEST

D_MODEL = 1024
D_INNER = 2048
HEAD_DIM = 64
HEADS = 32
GROUPS = 4
HEADS_PER_GROUP = HEADS // GROUPS
GROUP_WIDTH = HEADS_PER_GROUP * HEAD_DIM
STATE = 128
CHUNK = 128
CONV_DIM = D_INNER + 2 * GROUPS * STATE
SSD_CONV = 4
SC_CONV = 3
GRID_W = 64
N_EXPERTS = 16
EXPERTS_PER_GROUP = 4
D_EXPERT = 512
EPS = 1e-6
P_COLS = 10 * D_MODEL
DT_PAD = 128
ROW_TILE = 256
VMEM_LIMIT = 56 * 1024 * 1024


def _params(*sem):
    return pltpu.CompilerParams(dimension_semantics=sem, vmem_limit_bytes=VMEM_LIMIT)


def _silu(v):
    return v * jax.nn.sigmoid(v)


def _modulated_norm(x, gain, shift, scale):
    ms = jnp.mean(x * x, axis=-1, keepdims=True)
    return x * lax.rsqrt(ms + EPS) * gain * (1.0 + scale) + shift


def _ada_kernel(c_ref, w_ref, b_ref, o_ref):
    cond = _silu(c_ref[...])
    o_ref[...] = jnp.dot(cond, w_ref[...], precision=HIGHEST, preferred_element_type=F32) + b_ref[...]


def _ada_params(cc, w_ada, b_ada):
    depth, _, n = w_ada.shape
    rows = cc.shape[0]
    tn = 512
    return pl.pallas_call(
        _ada_kernel,
        grid=(depth, n // tn),
        in_specs=[pl.BlockSpec((rows, D_MODEL), lambda l, j: (0, 0)),
                  pl.BlockSpec((None, D_MODEL, tn), lambda l, j: (l, 0, j)),
                  pl.BlockSpec((None, 1, tn), lambda l, j: (l, 0, j))],
        out_specs=pl.BlockSpec((None, rows, tn), lambda l, j: (l, 0, j)),
        out_shape=jax.ShapeDtypeStruct((depth, rows, n), F32),
        compiler_params=_params("parallel", "parallel"),
        name="ada_params",
    )(cc, w_ada, b_ada.reshape(depth, 1, n))


def _in_proj_kernel(h_ref, mod_ref, gain_ref, w_ref, wdt_ref, dtb_ref, p_ref, dt_ref):
    u = _modulated_norm(h_ref[...], gain_ref[...], mod_ref[0:1, :], mod_ref[1:2, :]).astype(BF16)
    for n0 in range(0, P_COLS, 512):
        p_ref[:, n0:n0 + 512] = jnp.dot(u, w_ref[:, n0:n0 + 512], preferred_element_type=F32).astype(BF16)
    raw = jnp.dot(u, wdt_ref[...], preferred_element_type=F32) + dtb_ref[...]
    dt_ref[...] = jnp.maximum(raw, 0.0) + jnp.log(1.0 + jnp.exp(-jnp.abs(raw)))


def _in_proj(h, mods, gain, w_main, w_dt, dt_bias, mod_index):
    t = h.shape[0]
    return pl.pallas_call(
        _in_proj_kernel,
        grid=(t // ROW_TILE,),
        in_specs=[pl.BlockSpec((ROW_TILE, D_MODEL), lambda i: (i, 0)),
                  pl.BlockSpec((None, None, 6, D_MODEL), mod_index),
                  pl.BlockSpec((1, D_MODEL), lambda i: (0, 0)),
                  pl.BlockSpec((D_MODEL, P_COLS), lambda i: (0, 0), pipeline_mode=pl.Buffered(1)),
                  pl.BlockSpec((D_MODEL, DT_PAD), lambda i: (0, 0)),
                  pl.BlockSpec((1, DT_PAD), lambda i: (0, 0))],
        out_specs=[pl.BlockSpec((ROW_TILE, P_COLS), lambda i: (i, 0)),
                   pl.BlockSpec((ROW_TILE, DT_PAD), lambda i: (i, 0))],
        out_shape=[jax.ShapeDtypeStruct((t, P_COLS), BF16),
                   jax.ShapeDtypeStruct((t, DT_PAD), F32)],
        compiler_params=_params("parallel"),
        name="in_proj",
    )(h, mods, gain, w_main, w_dt, dt_bias)


def _ssd_conv_kernel(x_ref, w_ref, b_ref, o_ref, *, n_ctx):
    x = x_ref[...].astype(F32)
    s_len = x.shape[0]
    rows = lax.broadcasted_iota(jnp.int32, (s_len, 1), 0)
    acc = x * w_ref[1:2, :]
    for k in (0, 2, 3):
        off = k - 1
        src = rows + off
        ok = (src >= 0) & (src < s_len) & ((src >= n_ctx) == (rows >= n_ctx))
        shifted = pltpu.roll(x, (-off) % s_len, 0)
        acc = acc + jnp.where(ok, shifted, 0.0) * w_ref[k:k + 1, :]
    o_ref[...] = _silu(acc + b_ref[...]).astype(BF16)


def _ssd_conv(p3, conv_w, conv_b, n_ctx):
    b, s_len, _ = p3.shape
    tc = 256
    first = D_INNER // tc
    return pl.pallas_call(
        functools.partial(_ssd_conv_kernel, n_ctx=n_ctx),
        grid=(b, CONV_DIM // tc),
        in_specs=[pl.BlockSpec((None, s_len, tc), lambda i, c: (i, 0, first + c)),
                  pl.BlockSpec((SSD_CONV, tc), lambda i, c: (0, c)),
                  pl.BlockSpec((1, tc), lambda i, c: (0, c))],
        out_specs=pl.BlockSpec((None, s_len, tc), lambda i, c: (i, 0, c)),
        out_shape=jax.ShapeDtypeStruct((b, s_len, CONV_DIM), BF16),
        compiler_params=_params("parallel", "parallel"),
        name="ssd_conv",
    )(p3, conv_w, conv_b)


def _decay_kernel(dt_ref, alog_ref, a_ref, ea_ref, dw_ref, eal_ref):
    dt = dt_ref[...]
    da = dt * (-jnp.exp(alog_ref[...]))
    r = lax.broadcasted_iota(jnp.int32, (CHUNK, CHUNK), 0)
    c = lax.broadcasted_iota(jnp.int32, (CHUNK, CHUNK), 1)
    fwd = jnp.dot((c <= r).astype(F32), da, precision=HIGHEST, preferred_element_type=F32)
    bwd = jnp.dot((c >= r).astype(F32), da, precision=HIGHEST, preferred_element_type=F32)
    is_fwd = lax.broadcasted_iota(jnp.int32, (1, DT_PAD), 1) < HEADS
    a = jnp.where(is_fwd, fwd, bwd)
    total = jnp.where(is_fwd, fwd[CHUNK - 1:CHUNK, :], bwd[0:1, :])
    a_ref[...] = a
    ea_ref[...] = jnp.exp(a)
    dw_ref[...] = dt * jnp.exp(total - a)
    eal_ref[...] = jnp.exp(total)


def _decay_tables(dt3, alog):
    b, s_len, _ = dt3.shape
    nc = s_len // CHUNK
    tile = pl.BlockSpec((None, CHUNK, DT_PAD), lambda i, c: (i, c, 0))
    full = jax.ShapeDtypeStruct((b, s_len, DT_PAD), F32)
    return pl.pallas_call(
        _decay_kernel,
        grid=(b, nc),
        in_specs=[tile, pl.BlockSpec((1, DT_PAD), lambda i, c: (0, 0))],
        out_specs=[tile, tile, tile, pl.BlockSpec((None, None, 1, DT_PAD), lambda i, c: (i, c, 0, 0))],
        out_shape=[full, full, full, jax.ShapeDtypeStruct((b, nc, 1, DT_PAD), F32)],
        compiler_params=_params("parallel", "parallel"),
        name="ssd_decay",
    )(dt3, alog)


def _scan_kernel(x_ref, b_ref, c_ref, col_ref, row_ref, eal_ref, exp_ref, y_ref,
                 state_f, state_b, yacc, *, n_chunks, n_ctx_chunks):
    yacc[...] = jnp.zeros_like(yacc)
    state_f[...] = jnp.zeros_like(state_f)
    state_b[...] = jnp.zeros_like(state_b)
    r = lax.broadcasted_iota(jnp.int32, (CHUNK, CHUNK), 0)
    c = lax.broadcasted_iota(jnp.int32, (CHUNK, CHUNK), 1)
    masks = (r >= c, r <= c)
    low_half = lax.broadcasted_iota(jnp.int32, (1, 2 * HEAD_DIM), 1) < HEAD_DIM
    states = (state_f, state_b)

    def chunk_step(ci, d):
        r0 = pl.multiple_of(ci * CHUNK, CHUNK)
        xc = x_ref[pl.ds(r0, CHUNK), :]
        bc = b_ref[pl.ds(r0, CHUNK), :]
        cc = c_ref[pl.ds(r0, CHUNK), :]
        cols = col_ref[pl.ds(r0, CHUNK), :]
        arow = row_ref[d, 0, ci]
        dtrow = row_ref[d, 1, ci]
        cb = lax.dot_general(cc, bc, (((1,), (1,)), ((), ())), preferred_element_type=F32)
        spread = jnp.dot(cols.astype(BF16), exp_ref[d], preferred_element_type=F32)
        ea_x = spread[:, :GROUP_WIDTH]
        dw_x = spread[:, GROUP_WIDTH:]
        st = states[d][...]
        y = jnp.dot(cc, st.astype(BF16), preferred_element_type=F32) * ea_x
        parts = []
        for k in range(HEADS_PER_GROUP // 2):
            lhs = []
            for j in (2 * k, 2 * k + 1):
                col = 24 * d + j
                seg = cols[:, col:col + 1] - arow[j:j + 1, :]
                dec = jnp.exp(jnp.where(masks[d], seg, -jnp.inf))
                lhs.append((cb * dec * dtrow[j:j + 1, :]).astype(BF16))
            xp = xc[:, 128 * k:128 * (k + 1)]
            zero = jnp.zeros_like(xp)
            rhs = jnp.concatenate([jnp.where(low_half, xp, zero), jnp.where(low_half, zero, xp)], axis=0)
            parts.append(jnp.dot(jnp.concatenate(lhs, axis=1), rhs, preferred_element_type=F32))
        y = y + jnp.concatenate(parts, axis=1)
        yacc[pl.ds(r0, CHUNK), :] += y
        xw = (xc.astype(F32) * dw_x).astype(BF16)
        upd = lax.dot_general(bc, xw, (((0,), (0,)), ((), ())), preferred_element_type=F32)
        states[d][...] = st * eal_ref[ci, pl.ds(d, 1), :] + upd

    def body(i, carry):
        chunk_step(i, 0)
        cb_idx = jnp.where(i < n_ctx_chunks, n_ctx_chunks - 1 - i, n_chunks + n_ctx_chunks - 1 - i)
        chunk_step(cb_idx, 1)
        return carry

    lax.fori_loop(0, n_chunks, body, 0)
    y_ref[...] = yacc[...].astype(BF16)


def _ssd_scan(xbc, colslab, rowslab, ealx, spread_mat, n_ctx):
    b, s_len, _ = xbc.shape
    nc = s_len // CHUNK
    x_blocks = D_INNER // STATE
    return pl.pallas_call(
        functools.partial(_scan_kernel, n_chunks=nc, n_ctx_chunks=n_ctx // CHUNK),
        grid=(b, GROUPS),
        in_specs=[pl.BlockSpec((None, s_len, GROUP_WIDTH), lambda i, g: (i, 0, g)),
                  pl.BlockSpec((None, s_len, STATE), lambda i, g: (i, 0, x_blocks + g)),
                  pl.BlockSpec((None, s_len, STATE), lambda i, g: (i, 0, x_blocks + GROUPS + g)),
                  pl.BlockSpec((None, None, s_len, 48), lambda i, g: (i, g, 0, 0)),
                  pl.BlockSpec((None, None, 2, 2, nc, HEADS_PER_GROUP, CHUNK), lambda i, g: (i, g, 0, 0, 0, 0, 0)),
                  pl.BlockSpec((None, None, nc, 2, GROUP_WIDTH), lambda i, g: (i, g, 0, 0, 0)),
                  pl.BlockSpec((2, 48, 2 * GROUP_WIDTH), lambda i, g: (0, 0, 0))],
        out_specs=pl.BlockSpec((None, s_len, GROUP_WIDTH), lambda i, g: (i, 0, g)),
        out_shape=jax.ShapeDtypeStruct((b, s_len, D_INNER), BF16),
        scratch_shapes=[pltpu.VMEM((STATE, GROUP_WIDTH), F32),
                        pltpu.VMEM((STATE, GROUP_WIDTH), F32),
                        pltpu.VMEM((s_len, GROUP_WIDTH), F32)],
        compiler_params=_params("parallel", "parallel"),
        name="ssd_scan",
    )(xbc, xbc, xbc, colslab, rowslab, ealx, spread_mat)


def _spread_matrix():
    rows = jnp.arange(48)[:, None]
    lanes = jnp.arange(2 * GROUP_WIDTH)[None, :]
    mats = []
    for d in range(2):
        kind = (rows - 24 * d) // HEADS_PER_GROUP
        head = (rows - 24 * d) % HEADS_PER_GROUP
        valid = (rows >= 24 * d + 8) & (rows < 24 * d + 24)
        hit = valid & (lanes // GROUP_WIDTH == kind - 1) & ((lanes % GROUP_WIDTH) // HEAD_DIM == head)
        mats.append(hit.astype(BF16))
    return jnp.stack(mats)


def _mixer_out_kernel(h_ref, y_ref, x_ref, z_ref, scb_ref, scc_ref, sch_ref, ga_ref, gb_ref, mod_ref,
                      dskip_ref, gain_ref, wssd_ref, scw_ref, wsc_ref, wo_ref, o_ref, *, tiles_per_batch, n_ctx):
    yv = y_ref[...].astype(F32) + x_ref[...].astype(F32) * dskip_ref[...]
    yz = yv * _silu(z_ref[...].astype(F32))
    ms = jnp.mean(yz * yz, axis=-1, keepdims=True)
    yn = (yz * lax.rsqrt(ms + EPS) * gain_ref[...]).astype(BF16)
    ssd = jnp.dot(yn, wssd_ref[...], preferred_element_type=F32)

    v = scc_ref[...].astype(F32) * sch_ref[...].astype(F32)
    tm = v.shape[0]
    pos = (pl.program_id(0) % tiles_per_batch) * tm + lax.broadcasted_iota(jnp.int32, (tm, 1), 0)
    in_ctx = pos < n_ctx
    col = (pos - n_ctx) % GRID_W
    first = jnp.where(in_ctx, pos, col)
    last = jnp.where(in_ctx, n_ctx - 1 - pos, GRID_W - 1 - col)
    left_ok = first != 0
    right_ok = last != 0
    vl = jnp.where(left_ok, pltpu.roll(v, 1, 0), 0.0)
    vr = jnp.where(right_ok, pltpu.roll(v, tm - 1, 0), 0.0)
    cv = vl * scw_ref[0:1, :] + v * scw_ref[1:2, :] + vr * scw_ref[2:3, :]
    sc = jnp.dot((scb_ref[...].astype(F32) * cv).astype(BF16), wsc_ref[...], preferred_element_type=F32)

    mix = jax.nn.sigmoid(ga_ref[...].astype(F32)) * ssd + jax.nn.sigmoid(gb_ref[...].astype(F32)) * sc
    out = jnp.dot(mix.astype(BF16), wo_ref[...], preferred_element_type=F32)
    o_ref[...] = h_ref[...] + mod_ref[2:3, :] * out


def _mixer_out(h, y, xbc, p, mods, mod_index, d_skip, gain, w_ssd, sc_w, w_sc, w_o, tiles_per_batch, n_ctx):
    t = h.shape[0]
    tm = ROW_TILE
    row = lambda width, j: pl.BlockSpec((tm, width), lambda i: (i, j))
    const = lambda shape: pl.BlockSpec(shape, lambda i: (0,) * len(shape))
    return pl.pallas_call(
        functools.partial(_mixer_out_kernel, tiles_per_batch=tiles_per_batch, n_ctx=n_ctx),
        grid=(t // tm,),
        in_specs=[row(D_MODEL, 0), row(D_INNER, 0), row(D_INNER, 0), row(D_INNER, 0),
                  row(D_MODEL, 5), row(D_MODEL, 6), row(D_MODEL, 7), row(D_MODEL, 8), row(D_MODEL, 9),
                  pl.BlockSpec((None, None, 6, D_MODEL), mod_index),
                  const((1, D_INNER)), const((1, D_INNER)), const((D_INNER, D_MODEL)),
                  const((SC_CONV, D_MODEL)), const((D_MODEL, D_MODEL)), const((D_MODEL, D_MODEL))],
        out_specs=row(D_MODEL, 0),
        out_shape=jax.ShapeDtypeStruct((t, D_MODEL), F32),
        compiler_params=_params("parallel"),
        name="mixer_out",
    )(h, y, xbc, p, p, p, p, p, p, mods, d_skip, gain, w_ssd, sc_w, w_sc, w_o)


def _router_kernel(h_ref, mod_ref, gain_ref, wr_ref, br_ref, u_ref, idx_ref, wts_ref):
    u = _modulated_norm(h_ref[...], gain_ref[...], mod_ref[3:4, :], mod_ref[4:5, :])
    u_ref[...] = u
    logits = lax.dot_general(wr_ref[...], u, (((1,), (1,)), ((), ())), precision=HIGHEST,
                             preferred_element_type=F32)
    scores = jax.nn.sigmoid(logits)
    sel = scores + br_ref[...]
    best_val = None
    best_group = None
    for g in range(N_EXPERTS // EXPERTS_PER_GROUP):
        v = [sel[EXPERTS_PER_GROUP * g + i:EXPERTS_PER_GROUP * g + i + 1, :] for i in range(EXPERTS_PER_GROUP)]
        top2 = None
        for i in range(EXPERTS_PER_GROUP):
            for j in range(i + 1, EXPERTS_PER_GROUP):
                pair = v[i] + v[j]
                top2 = pair if top2 is None else jnp.maximum(top2, pair)
        if g == 0:
            best_val, best_group = top2, jnp.zeros_like(top2, dtype=jnp.int32)
        else:
            better = top2 > best_val
            best_group = jnp.where(better, g, best_group)
            best_val = jnp.where(better, top2, best_val)
    eidx = lax.broadcasted_iota(jnp.int32, sel.shape, 0)
    masked = jnp.where(eidx // EXPERTS_PER_GROUP == best_group, sel, -jnp.inf)
    m1 = jnp.max(masked, axis=0, keepdims=True)
    i1 = jnp.min(jnp.where(masked == m1, eidx, N_EXPERTS), axis=0, keepdims=True)
    masked2 = jnp.where(eidx == i1, -jnp.inf, masked)
    m2 = jnp.max(masked2, axis=0, keepdims=True)
    i2 = jnp.min(jnp.where(masked2 == m2, eidx, N_EXPERTS), axis=0, keepdims=True)
    w1 = jnp.sum(jnp.where(eidx == i1, scores, 0.0), axis=0, keepdims=True)
    w2 = jnp.sum(jnp.where(eidx == i2, scores, 0.0), axis=0, keepdims=True)
    denom = w1 + w2
    slot = lax.broadcasted_iota(jnp.int32, idx_ref.shape, 0)
    idx_ref[...] = jnp.where(slot == 0, i1, jnp.where(slot == 1, i2, 0))
    wts_ref[...] = jnp.where(slot == 0, w1 / denom, jnp.where(slot == 1, w2 / denom, 0.0))


def _router(h, mods, mod_index, gain, wr_t, b_router):
    t = h.shape[0]
    tm = ROW_TILE
    return pl.pallas_call(
        _router_kernel,
        grid=(t // tm,),
        in_specs=[pl.BlockSpec((tm, D_MODEL), lambda i: (i, 0)),
                  pl.BlockSpec((None, None, 6, D_MODEL), mod_index),
                  pl.BlockSpec((1, D_MODEL), lambda i: (0, 0)),
                  pl.BlockSpec((N_EXPERTS, D_MODEL), lambda i: (0, 0)),
                  pl.BlockSpec((N_EXPERTS, 1), lambda i: (0, 0))],
        out_specs=[pl.BlockSpec((tm, D_MODEL), lambda i: (i, 0)),
                   pl.BlockSpec((8, tm), lambda i: (0, i)),
                   pl.BlockSpec((8, tm), lambda i: (0, i))],
        out_shape=[jax.ShapeDtypeStruct((t, D_MODEL), F32),
                   jax.ShapeDtypeStruct((8, t), jnp.int32),
                   jax.ShapeDtypeStruct((8, t), F32)],
        compiler_params=_params("parallel"),
        name="router",
    )(h, mods, gain, wr_t, b_router)


def _gather_tile(i, n, idx_hbm, src_hbm, idx_smem, rows, sem_idx, sem_rows):
    def idx_copy(k):
        return pltpu.make_async_copy(idx_hbm.at[k], idx_smem.at[k % 2], sem_idx.at[k % 2])

    def row_copy(k, r, src_row):
        return pltpu.make_async_copy(src_hbm.at[pl.ds(src_row, 1)], rows.at[k % 2, pl.ds(r, 1)], sem_rows.at[k % 2])

    def issue_rows(k):
        def body(r, carry):
            row_copy(k, r, idx_smem[k % 2, r]).start()
            return carry
        lax.fori_loop(0, ROW_TILE, body, 0, unroll=8)

    def wait_rows(k):
        def body(r, carry):
            row_copy(k, r, 0).wait()
            return carry
        lax.fori_loop(0, ROW_TILE, body, 0, unroll=8)

    @pl.when((i == 0) & (n > 0))
    def _():
        idx_copy(0).start()
        idx_copy(0).wait()
        issue_rows(0)

    @pl.when((i == 0) & (n > 1))
    def _():
        idx_copy(1).start()

    @pl.when(i + 1 < n)
    def _():
        idx_copy(i + 1).wait()
        issue_rows(i + 1)

    @pl.when(i + 2 < n)
    def _():
        idx_copy(i + 2).start()

    @pl.when(i < n)
    def _():
        wait_rows(i)


def _experts_kernel(lo_ref, hi_ref, n_ref, idx_hbm, u_hbm, wts_ref, wg_lo, wu_lo, wd_lo, wg_hi, wu_hi, wd_hi,
                    y_ref, idx_smem, rows, sem_idx, sem_rows):
    i = pl.program_id(0)
    n = n_ref[0]
    _gather_tile(i, n, idx_hbm, u_hbm, idx_smem, rows, sem_idx, sem_rows)

    @pl.when(i < n)
    def _():
        u = rows[i % 2].astype(BF16)

        def expert(wg, wu, wd):
            hid = _silu(jnp.dot(u, wg[...], preferred_element_type=F32)) * jnp.dot(u, wu[...], preferred_element_type=F32)
            return jnp.dot(hid.astype(BF16), wd[...], preferred_element_type=F32)

        y_ref[...] = wts_ref[:, 0:1] * expert(wg_lo, wu_lo, wd_lo) + wts_ref[:, 1:2] * expert(wg_hi, wu_hi, wd_hi)

    @pl.when(i >= n)
    def _():
        y_ref[...] = jnp.zeros_like(y_ref)


def _experts(tile_lo, tile_hi, n_used, src_idx, u, wts_sorted, w_gate, w_up, w_down):
    n_tiles = src_idx.shape[0]
    tm = ROW_TILE
    w_in_spec = lambda which: pl.BlockSpec((None, D_MODEL, D_EXPERT),
                                           lambda i, lo, hi, n: ((lo, hi)[which][i], 0, 0))
    w_out_spec = lambda which: pl.BlockSpec((None, D_EXPERT, D_MODEL),
                                            lambda i, lo, hi, n: ((lo, hi)[which][i], 0, 0))
    return pl.pallas_call(
        _experts_kernel,
        grid_spec=pltpu.PrefetchScalarGridSpec(
            num_scalar_prefetch=3,
            grid=(n_tiles,),
            in_specs=[pl.BlockSpec(memory_space=pl.ANY),
                      pl.BlockSpec(memory_space=pl.ANY),
                      pl.BlockSpec((tm, 2), lambda i, lo, hi, n: (i, 0)),
                      w_in_spec(0), w_in_spec(0), w_out_spec(0),
                      w_in_spec(1), w_in_spec(1), w_out_spec(1)],
            out_specs=pl.BlockSpec((tm, D_MODEL), lambda i, lo, hi, n: (i, 0)),
            scratch_shapes=[pltpu.SMEM((2, tm), jnp.int32),
                            pltpu.VMEM((2, tm, D_MODEL), F32),
                            pltpu.SemaphoreType.DMA((2,)),
                            pltpu.SemaphoreType.DMA((2,))]),
        out_shape=jax.ShapeDtypeStruct((n_tiles * tm, D_MODEL), F32),
        compiler_params=_params("arbitrary"),
        name="experts",
    )(tile_lo, tile_hi, n_used, src_idx, u, wts_sorted, w_gate, w_up, w_down, w_gate, w_up, w_down)


def _combine_kernel(pos_hbm, y_hbm, h_ref, mod_ref, o_ref, idx_smem, rows, sem_idx, sem_rows):
    i = pl.program_id(0)
    _gather_tile(i, pl.num_programs(0), pos_hbm, y_hbm, idx_smem, rows, sem_idx, sem_rows)
    o_ref[...] = h_ref[...] + mod_ref[5:6, :] * rows[i % 2]


def _combine(pos, y_sorted, h, mods, mod_index):
    t = h.shape[0]
    tm = ROW_TILE
    return pl.pallas_call(
        _combine_kernel,
        grid=(t // tm,),
        in_specs=[pl.BlockSpec(memory_space=pl.ANY),
                  pl.BlockSpec(memory_space=pl.ANY),
                  pl.BlockSpec((tm, D_MODEL), lambda i: (i, 0)),
                  pl.BlockSpec((None, None, 6, D_MODEL), mod_index)],
        out_specs=pl.BlockSpec((tm, D_MODEL), lambda i: (i, 0)),
        out_shape=jax.ShapeDtypeStruct((t, D_MODEL), F32),
        scratch_shapes=[pltpu.SMEM((2, tm), jnp.int32),
                        pltpu.VMEM((2, tm, D_MODEL), F32),
                        pltpu.SemaphoreType.DMA((2,)),
                        pltpu.SemaphoreType.DMA((2,))],
        compiler_params=_params("arbitrary"),
        name="moe_combine",
    )(pos.reshape(t // tm, tm), y_sorted, h, mods)


N_PAIRS = EXPERTS_PER_GROUP * (EXPERTS_PER_GROUP - 1) // 2
N_CLASSES = (N_EXPERTS // EXPERTS_PER_GROUP) * N_PAIRS


def _routing_tables(idx, wts, t):
    tm = ROW_TILE
    n_tiles = t // tm + N_CLASSES
    i1, i2 = idx[0], idx[1]
    lo, hi = jnp.minimum(i1, i2), jnp.maximum(i1, i2)
    first_is_lo = i1 < i2
    w_lo = jnp.where(first_is_lo, wts[0], wts[1])
    w_hi = jnp.where(first_is_lo, wts[1], wts[0])
    pairs = [(a, b) for a in range(EXPERTS_PER_GROUP) for b in range(a + 1, EXPERTS_PER_GROUP)]
    pair_of = [[0] * EXPERTS_PER_GROUP for _ in range(EXPERTS_PER_GROUP)]
    for k, (a, b) in enumerate(pairs):
        pair_of[a][b] = k
    pair_tab = jnp.asarray(pair_of, jnp.int32).reshape(-1)
    cls = (lo // EXPERTS_PER_GROUP) * N_PAIRS + pair_tab[(lo % EXPERTS_PER_GROUP) * EXPERTS_PER_GROUP
                                                         + hi % EXPERTS_PER_GROUP]
    onehot = (cls[:, None] == jnp.arange(N_CLASSES, dtype=jnp.int32)[None, :]).astype(jnp.int32)
    running = jnp.cumsum(onehot, axis=0)
    rank = jnp.sum(running * onehot, axis=1) - 1
    counts = running[-1]
    padded = (counts + tm - 1) // tm * tm
    ends = jnp.cumsum(padded)
    pos = (ends - padded)[cls] + rank
    n_used = (ends[-1] // tm).astype(jnp.int32).reshape(1)
    src = jnp.zeros((n_tiles * tm,), jnp.int32).at[pos].set(jnp.arange(t, dtype=jnp.int32))
    wts_sorted = jnp.zeros((n_tiles * tm, 2), F32).at[pos].set(jnp.stack([w_lo, w_hi], axis=1))
    tile_first = jnp.minimum(jnp.arange(n_tiles, dtype=jnp.int32), n_used[0] - 1) * tm
    tile_cls = jnp.minimum(jnp.sum((tile_first[:, None] >= ends[None, :]).astype(jnp.int32), axis=1), N_CLASSES - 1)
    class_lo = jnp.asarray([g * EXPERTS_PER_GROUP + a for g in range(N_EXPERTS // EXPERTS_PER_GROUP)
                            for a, _ in pairs], jnp.int32)
    class_hi = jnp.asarray([g * EXPERTS_PER_GROUP + b for g in range(N_EXPERTS // EXPERTS_PER_GROUP)
                            for _, b in pairs], jnp.int32)
    return class_lo[tile_cls], class_hi[tile_cls], n_used, src.reshape(n_tiles, tm), wts_sorted, pos


def _final_norm_kernel(h_ref, gain_ref, o_ref):
    x = h_ref[...]
    ms = jnp.mean(x * x, axis=-1, keepdims=True)
    o_ref[...] = x * lax.rsqrt(ms + EPS) * gain_ref[...]


def _final_norm(h3, gain, n_ctx):
    b, s_len, _ = h3.shape
    tm = ROW_TILE
    skip = n_ctx // tm
    return pl.pallas_call(
        _final_norm_kernel,
        grid=(b, (s_len - n_ctx) // tm),
        in_specs=[pl.BlockSpec((None, tm, D_MODEL), lambda i, j: (i, skip + j, 0)),
                  pl.BlockSpec((1, D_MODEL), lambda i, j: (0, 0))],
        out_specs=pl.BlockSpec((None, tm, D_MODEL), lambda i, j: (i, j, 0)),
        out_shape=jax.ShapeDtypeStruct((b, s_len - n_ctx, D_MODEL), F32),
        compiler_params=_params("parallel", "parallel"),
        name="final_norm",
    )(h3, gain)


def _group_major(t, b, s_len):
    return t[:, :, :2 * HEADS].reshape(b, s_len, 2, GROUPS, HEADS_PER_GROUP)


def kernel(x, c, ctx, c_ctx, w_ada, b_ada, norm_mix, norm_ffn, w_in, ssd_conv_w, ssd_conv_b, ssd_dt_bias,
           ssd_a_log, ssd_d, ssd_norm, w_ssd_out, sc_conv_w, w_sc_out, w_o, w_router, b_router, w_gate, w_up,
           w_down, final_norm):
    b, seq, _ = x.shape
    n_ctx = ctx.shape[1]
    depth = w_ada.shape[0]
    s_len = n_ctx + seq
    t = b * s_len
    nc = s_len // CHUNK
    assert n_ctx == ROW_TILE and seq % ROW_TILE == 0 and ROW_TILE % GRID_W == 0 and ROW_TILE % CHUNK == 0
    tiles_per_batch = s_len // ROW_TILE

    def mod_index(i):
        return (i // tiles_per_batch, jnp.minimum(i % tiles_per_batch, 1), 0, 0)

    rows = -(-(b + 1) // 8) * 8
    cc = jnp.zeros((rows, D_MODEL), F32).at[:b].set(c).at[b].set(c_ctx)
    ada = _ada_params(cc, w_ada, b_ada)
    lat = ada[:, :b].reshape(depth, b, 1, 6, D_MODEL)
    con = jnp.broadcast_to(ada[:, b].reshape(depth, 1, 1, 6, D_MODEL), lat.shape)
    mods_all = jnp.concatenate([con, lat], axis=2)

    spread_mat = _spread_matrix()
    wr_t = w_router.T
    br = b_router.reshape(N_EXPERTS, 1)
    h = jnp.concatenate([ctx, x], axis=1).reshape(t, D_MODEL)

    for l in range(depth):
        mods = mods_all[l]
        w_l = w_in[l]
        w_main = jnp.concatenate([w_l[:, :D_INNER + CONV_DIM], w_l[:, D_INNER + CONV_DIM + 2 * HEADS:]],
                                 axis=1).astype(BF16)
        w_dt = jnp.pad(w_l[:, D_INNER + CONV_DIM:D_INNER + CONV_DIM + 2 * HEADS],
                       ((0, 0), (0, DT_PAD - 2 * HEADS))).astype(BF16)
        dt_bias = jnp.pad(ssd_dt_bias[l].reshape(1, 2 * HEADS), ((0, 0), (0, DT_PAD - 2 * HEADS)))
        alog = jnp.pad(ssd_a_log[l].reshape(1, 2 * HEADS), ((0, 0), (0, DT_PAD - 2 * HEADS)))

        p, dt = _in_proj(h, mods, norm_mix[l].reshape(1, D_MODEL), w_main, w_dt, dt_bias, mod_index)
        xbc = _ssd_conv(p.reshape(b, s_len, P_COLS), ssd_conv_w[l], ssd_conv_b[l].reshape(1, CONV_DIM), n_ctx)
        dt3 = dt.reshape(b, s_len, DT_PAD)
        a, ea, dw, eal = _decay_tables(dt3, alog)

        cols = jnp.stack([_group_major(v, b, s_len) for v in (a, ea, dw)], axis=3)
        colslab = cols.transpose(0, 4, 1, 2, 3, 5).reshape(b, GROUPS, s_len, 48)
        rows_t = jnp.stack([_group_major(v, b, s_len) for v in (a, dt3)], axis=3)
        rowslab = rows_t.reshape(b, nc, CHUNK, 2, 2, GROUPS, HEADS_PER_GROUP).transpose(0, 5, 3, 4, 1, 6, 2)
        ealx = eal[:, :, 0, :2 * HEADS].reshape(b, nc, 2, GROUPS, HEADS_PER_GROUP).transpose(0, 3, 1, 2, 4)
        ealx = jnp.repeat(ealx, HEAD_DIM, axis=-1)

        y = _ssd_scan(xbc, colslab, rowslab, ealx, spread_mat, n_ctx)
        h = _mixer_out(h, y.reshape(t, D_INNER), xbc.reshape(t, CONV_DIM), p, mods, mod_index,
                       jnp.repeat(ssd_d[l], HEAD_DIM).reshape(1, D_INNER), ssd_norm[l].reshape(1, D_INNER),
                       w_ssd_out[l].astype(BF16), sc_conv_w[l], w_sc_out[l].astype(BF16), w_o[l].astype(BF16),
                       tiles_per_batch, n_ctx)
        u2, idx, wts = _router(h, mods, mod_index, norm_ffn[l].reshape(1, D_MODEL), wr_t, br)
        tile_lo, tile_hi, n_used, src_idx, wts_sorted, pos = _routing_tables(idx, wts, t)
        y_sorted = _experts(tile_lo, tile_hi, n_used, src_idx, u2, wts_sorted,
                            w_gate[l].astype(BF16), w_up[l].astype(BF16), w_down[l].astype(BF16))
        h = _combine(pos, y_sorted, h, mods, mod_index)

    return _final_norm(h.reshape(b, s_len, D_MODEL), final_norm.reshape(1, D_MODEL), n_ctx)
```

```python
import functools

import jax
import jax.numpy as jnp
from jax import lax
from jax.experimental import pallas as pl
from jax.experimental.pallas import tpu as pltpu

F32 = jnp.float32
BF16 = jnp.bfloat16
HIGHEST = lax.Precision.HIGHEST

D_MODEL = 1024
D_INNER = 2048
HEAD_DIM = 64
HEADS = 32
GROUPS = 4
HEADS_PER_GROUP = HEADS // GROUPS
GROUP_WIDTH = HEADS_PER_GROUP * HEAD_DIM
STATE = 128
CHUNK = 128
CONV_DIM = D_INNER + 2 * GROUPS * STATE
SSD_CONV = 4
SC_CONV = 3
GRID_W = 64
N_EXPERTS = 16
EXPERTS_PER_GROUP = 4
D_EXPERT = 512
EPS = 1e-6
P_COLS = 10 * D_MODEL
DT_PAD = 128
ROW_TILE = 256
VMEM_LIMIT = 56 * 1024 * 1024


def _params(*sem):
    return pltpu.CompilerParams(dimension_semantics=sem, vmem_limit_bytes=VMEM_LIMIT)


def _silu(v):
    return v * jax.nn.sigmoid(v)


def _modulated_norm(x, gain, shift, scale):
    ms = jnp.mean(x * x, axis=-1, keepdims=True)
    return x * lax.rsqrt(ms + EPS) * gain * (1.0 + scale) + shift


def _ada_kernel(c_ref, w_ref, b_ref, o_ref):
    cond = _silu(c_ref[...])
    o_ref[...] = jnp.dot(cond, w_ref[...], precision=HIGHEST, preferred_element_type=F32) + b_ref[...]


def _ada_params(cc, w_ada, b_ada):
    depth, _, n = w_ada.shape
    rows = cc.shape[0]
    tn = 512
    return pl.pallas_call(
        _ada_kernel,
        grid=(depth, n // tn),
        in_specs=[pl.BlockSpec((rows, D_MODEL), lambda l, j: (0, 0)),
                  pl.BlockSpec((None, D_MODEL, tn), lambda l, j: (l, 0, j)),
                  pl.BlockSpec((None, 1, tn), lambda l, j: (l, 0, j))],
        out_specs=pl.BlockSpec((None, rows, tn), lambda l, j: (l, 0, j)),
        out_shape=jax.ShapeDtypeStruct((depth, rows, n), F32),
        compiler_params=_params("parallel", "parallel"),
        name="ada_params",
    )(cc, w_ada, b_ada.reshape(depth, 1, n))


def _in_proj_kernel(h_ref, mod_ref, gain_ref, w_ref, wdt_ref, dtb_ref, p_ref, dt_ref):
    u = _modulated_norm(h_ref[...], gain_ref[...], mod_ref[0:1, :], mod_ref[1:2, :]).astype(BF16)
    for n0 in range(0, P_COLS, 512):
        p_ref[:, n0:n0 + 512] = jnp.dot(u, w_ref[:, n0:n0 + 512], preferred_element_type=F32).astype(BF16)
    raw = jnp.dot(u, wdt_ref[...], preferred_element_type=F32) + dtb_ref[...]
    dt_ref[...] = jnp.maximum(raw, 0.0) + jnp.log(1.0 + jnp.exp(-jnp.abs(raw)))


def _in_proj(h, mods, gain, w_main, w_dt, dt_bias, mod_index):
    t = h.shape[0]
    return pl.pallas_call(
        _in_proj_kernel,
        grid=(t // ROW_TILE,),
        in_specs=[pl.BlockSpec((ROW_TILE, D_MODEL), lambda i: (i, 0)),
                  pl.BlockSpec((None, None, 6, D_MODEL), mod_index),
                  pl.BlockSpec((1, D_MODEL), lambda i: (0, 0)),
                  pl.BlockSpec((D_MODEL, P_COLS), lambda i: (0, 0), pipeline_mode=pl.Buffered(1)),
                  pl.BlockSpec((D_MODEL, DT_PAD), lambda i: (0, 0)),
                  pl.BlockSpec((1, DT_PAD), lambda i: (0, 0))],
        out_specs=[pl.BlockSpec((ROW_TILE, P_COLS), lambda i: (i, 0)),
                   pl.BlockSpec((ROW_TILE, DT_PAD), lambda i: (i, 0))],
        out_shape=[jax.ShapeDtypeStruct((t, P_COLS), BF16),
                   jax.ShapeDtypeStruct((t, DT_PAD), F32)],
        compiler_params=_params("parallel"),
        name="in_proj",
    )(h, mods, gain, w_main, w_dt, dt_bias)


def _ssd_conv_kernel(x_ref, w_ref, b_ref, o_ref, *, n_ctx):
    x = x_ref[...].astype(F32)
    s_len = x.shape[0]
    rows = lax.broadcasted_iota(jnp.int32, (s_len, 1), 0)
    acc = x * w_ref[1:2, :]
    for k in (0, 2, 3):
        off = k - 1
        src = rows + off
        ok = (src >= 0) & (src < s_len) & ((src >= n_ctx) == (rows >= n_ctx))
        shifted = pltpu.roll(x, (-off) % s_len, 0)
        acc = acc + jnp.where(ok, shifted, 0.0) * w_ref[k:k + 1, :]
    o_ref[...] = _silu((acc + b_ref[...]).astype(BF16))


def _ssd_conv(p3, conv_w, conv_b, n_ctx):
    b, s_len, _ = p3.shape
    tc = 256
    first = D_INNER // tc
    return pl.pallas_call(
        functools.partial(_ssd_conv_kernel, n_ctx=n_ctx),
        grid=(b, CONV_DIM // tc),
        in_specs=[pl.BlockSpec((None, s_len, tc), lambda i, c: (i, 0, first + c)),
                  pl.BlockSpec((SSD_CONV, tc), lambda i, c: (0, c)),
                  pl.BlockSpec((1, tc), lambda i, c: (0, c))],
        out_specs=pl.BlockSpec((None, s_len, tc), lambda i, c: (i, 0, c)),
        out_shape=jax.ShapeDtypeStruct((b, s_len, CONV_DIM), BF16),
        compiler_params=_params("parallel", "parallel"),
        name="ssd_conv",
    )(p3, conv_w, conv_b)


LOG2E = 1.4426950408889634


def _decay_kernel(dt_ref, alog_ref, a2_ref, arow_ref, ea_ref, dw_ref, eal_ref, *, n_chunks):
    neg_a = -jnp.exp(alog_ref[...])
    r = lax.broadcasted_iota(jnp.int32, (CHUNK, CHUNK), 0)
    c = lax.broadcasted_iota(jnp.int32, (CHUNK, CHUNK), 1)
    lower = (c <= r).astype(F32)
    upper = (c >= r).astype(F32)
    is_fwd = lax.broadcasted_iota(jnp.int32, (1, DT_PAD), 1) < HEADS

    def body(ci, carry):
        rows = pl.ds(pl.multiple_of(ci * CHUNK, CHUNK), CHUNK)
        dt = dt_ref[rows, :]
        da = dt * neg_a
        fwd = jnp.dot(lower, da, precision=HIGHEST, preferred_element_type=F32)
        bwd = jnp.dot(upper, da, precision=HIGHEST, preferred_element_type=F32)
        a = jnp.where(is_fwd, fwd, bwd)
        total = jnp.where(is_fwd, fwd[CHUNK - 1:CHUNK, :], bwd[0:1, :])
        a2 = a * LOG2E
        a2_ref[rows, :] = a2
        arow_ref[rows, :] = a2 - jnp.log(dt) * LOG2E
        ea_ref[rows, :] = jnp.exp(a)
        dw_ref[rows, :] = dt * jnp.exp(total - a)
        eal_ref[ci] = jnp.exp(total)
        return carry

    lax.fori_loop(0, n_chunks, body, 0)


def _decay_tables(dt3, alog):
    b, s_len, _ = dt3.shape
    nc = s_len // CHUNK
    tile = pl.BlockSpec((None, s_len, DT_PAD), lambda i: (i, 0, 0))
    full = jax.ShapeDtypeStruct((b, s_len, DT_PAD), F32)
    return pl.pallas_call(
        functools.partial(_decay_kernel, n_chunks=nc),
        grid=(b,),
        in_specs=[tile, pl.BlockSpec((1, DT_PAD), lambda i: (0, 0))],
        out_specs=[tile, tile, tile, tile, pl.BlockSpec((None, nc, 1, DT_PAD), lambda i: (i, 0, 0, 0))],
        out_shape=[full, full, full, full, jax.ShapeDtypeStruct((b, nc, 1, DT_PAD), F32)],
        compiler_params=_params("parallel"),
        name="ssd_decay",
    )(dt3, alog)


def _scan_kernel(x_ref, b_ref, c_ref, col_ref, row_ref, eal_ref, exp_ref, y_ref,
                 state_f, state_b, yacc, *, n_chunks, n_ctx_chunks):
    yacc[...] = jnp.zeros_like(yacc)
    state_f[...] = jnp.zeros_like(state_f)
    state_b[...] = jnp.zeros_like(state_b)
    r = lax.broadcasted_iota(jnp.int32, (CHUNK, CHUNK), 0)
    c = lax.broadcasted_iota(jnp.int32, (CHUNK, CHUNK), 1)
    mask_bias = tuple(jnp.where(keep, 0.0, -jnp.inf).astype(BF16) for keep in (r >= c, r <= c))
    low_half = lax.broadcasted_iota(jnp.int32, (1, 2 * HEAD_DIM), 1) < HEAD_DIM
    states = (state_f, state_b)

    def chunk_step(ci, d):
        r0 = pl.multiple_of(ci * CHUNK, CHUNK)
        xc = x_ref[pl.ds(r0, CHUNK), :]
        bc = b_ref[pl.ds(r0, CHUNK), :]
        cc = c_ref[pl.ds(r0, CHUNK), :]
        cols = col_ref[pl.ds(r0, CHUNK), :]
        arow = row_ref[d, ci]
        cb = lax.dot_general(cc, bc, (((1,), (1,)), ((), ())), preferred_element_type=F32).astype(BF16)
        spread = jnp.dot(cols.astype(BF16), exp_ref[d], preferred_element_type=F32)
        ea_x = spread[:, :GROUP_WIDTH]
        dw_x = spread[:, GROUP_WIDTH:]
        st = states[d][...]
        y = jnp.dot(cc, st.astype(BF16), preferred_element_type=F32) * ea_x
        parts = []
        for k in range(HEADS_PER_GROUP // 2):
            lhs = []
            for j in (2 * k, 2 * k + 1):
                col = 24 * d + j
                seg = (cols[:, col:col + 1] - arow[j:j + 1, :]).astype(BF16)
                lhs.append(jnp.exp2(seg + mask_bias[d]) * cb)
            xp = xc[:, 128 * k:128 * (k + 1)]
            zero = jnp.zeros_like(xp)
            rhs = jnp.concatenate([jnp.where(low_half, xp, zero), jnp.where(low_half, zero, xp)], axis=0)
            parts.append(jnp.dot(jnp.concatenate(lhs, axis=1), rhs, preferred_element_type=F32))
        y = y + jnp.concatenate(parts, axis=1)
        yacc[pl.ds(r0, CHUNK), :] += y
        xw = (xc.astype(F32) * dw_x).astype(BF16)
        upd = lax.dot_general(bc, xw, (((0,), (0,)), ((), ())), preferred_element_type=F32)
        states[d][...] = st * eal_ref[ci, pl.ds(d, 1), :] + upd

    def body(i, carry):
        chunk_step(i, 0)
        cb_idx = jnp.where(i < n_ctx_chunks, n_ctx_chunks - 1 - i, n_chunks + n_ctx_chunks - 1 - i)
        chunk_step(cb_idx, 1)
        return carry

    lax.fori_loop(0, n_chunks, body, 0)
    y_ref[...] = yacc[...].astype(BF16)


def _ssd_scan(xbc, colslab, rowslab, ealx, spread_mat, n_ctx):
    b, s_len, _ = xbc.shape
    nc = s_len // CHUNK
    x_blocks = D_INNER // STATE
    return pl.pallas_call(
        functools.partial(_scan_kernel, n_chunks=nc, n_ctx_chunks=n_ctx // CHUNK),
        grid=(b, GROUPS),
        in_specs=[pl.BlockSpec((None, s_len, GROUP_WIDTH), lambda i, g: (i, 0, g)),
                  pl.BlockSpec((None, s_len, STATE), lambda i, g: (i, 0, x_blocks + g)),
                  pl.BlockSpec((None, s_len, STATE), lambda i, g: (i, 0, x_blocks + GROUPS + g)),
                  pl.BlockSpec((None, None, s_len, 48), lambda i, g: (i, g, 0, 0)),
                  pl.BlockSpec((None, None, 2, nc, HEADS_PER_GROUP, CHUNK), lambda i, g: (i, g, 0, 0, 0, 0)),
                  pl.BlockSpec((None, None, nc, 2, GROUP_WIDTH), lambda i, g: (i, g, 0, 0, 0)),
                  pl.BlockSpec((2, 48, 2 * GROUP_WIDTH), lambda i, g: (0, 0, 0))],
        out_specs=pl.BlockSpec((None, s_len, GROUP_WIDTH), lambda i, g: (i, 0, g)),
        out_shape=jax.ShapeDtypeStruct((b, s_len, D_INNER), BF16),
        scratch_shapes=[pltpu.VMEM((STATE, GROUP_WIDTH), F32),
                        pltpu.VMEM((STATE, GROUP_WIDTH), F32),
                        pltpu.VMEM((s_len, GROUP_WIDTH), F32)],
        compiler_params=_params("parallel", "parallel"),
        name="ssd_scan",
    )(xbc, xbc, xbc, colslab, rowslab, ealx, spread_mat)


def _spread_matrix():
    rows = jnp.arange(48)[:, None]
    lanes = jnp.arange(2 * GROUP_WIDTH)[None, :]
    mats = []
    for d in range(2):
        kind = (rows - 24 * d) // HEADS_PER_GROUP
        head = (rows - 24 * d) % HEADS_PER_GROUP
        valid = (rows >= 24 * d + 8) & (rows < 24 * d + 24)
        hit = valid & (lanes // GROUP_WIDTH == kind - 1) & ((lanes % GROUP_WIDTH) // HEAD_DIM == head)
        mats.append(hit.astype(BF16))
    return jnp.stack(mats)


def _mixer_out_kernel(h_ref, y_ref, x_ref, z_ref, scb_ref, scc_ref, sch_ref, ga_ref, gb_ref, mod_ref,
                      dskip_ref, gain_ref, wssd_ref, scw_ref, wsc_ref, wo_ref, o_ref, *, tiles_per_batch, n_ctx):
    v = scc_ref[...].astype(F32) * sch_ref[...].astype(F32)
    tm = v.shape[0]
    pos = (pl.program_id(0) % tiles_per_batch) * tm + lax.broadcasted_iota(jnp.int32, (tm, 1), 0)
    in_ctx = pos < n_ctx
    col = (pos - n_ctx) % GRID_W
    first = jnp.where(in_ctx, pos, col)
    last = jnp.where(in_ctx, n_ctx - 1 - pos, GRID_W - 1 - col)
    left_ok = first != 0
    right_ok = last != 0
    vl = jnp.where(left_ok, pltpu.roll(v, 1, 0), 0.0)
    vr = jnp.where(right_ok, pltpu.roll(v, tm - 1, 0), 0.0)
    cv = vl * scw_ref[0:1, :] + v * scw_ref[1:2, :] + vr * scw_ref[2:3, :]
    gated = (scb_ref[...].astype(F32) * cv).astype(BF16)

    half = tm // 2
    for r0 in (0, half):
        rows = slice(r0, r0 + half)
        yv = y_ref[rows, :].astype(F32) + x_ref[rows, :].astype(F32) * dskip_ref[...]
        yz = yv * _silu(z_ref[rows, :]).astype(F32)
        ms = jnp.mean(yz * yz, axis=-1, keepdims=True)
        yn = (yz * lax.rsqrt(ms + EPS) * gain_ref[...]).astype(BF16)
        ssd = jnp.dot(yn, wssd_ref[...], preferred_element_type=F32)
        sc = jnp.dot(gated[rows, :], wsc_ref[...], preferred_element_type=F32)
        mix = (jax.nn.sigmoid(ga_ref[rows, :]) * ssd.astype(BF16)
               + jax.nn.sigmoid(gb_ref[rows, :]) * sc.astype(BF16))
        out = jnp.dot(mix, wo_ref[...], preferred_element_type=F32)
        o_ref[rows, :] = h_ref[rows, :] + mod_ref[2:3, :] * out


def _mixer_out(h, y, xbc, p, mods, mod_index, d_skip, gain, w_ssd, sc_w, w_sc, w_o, tiles_per_batch, n_ctx):
    t = h.shape[0]
    tm = ROW_TILE
    row = lambda width, j: pl.BlockSpec((tm, width), lambda i: (i, j))
    const = lambda shape: pl.BlockSpec(shape, lambda i: (0,) * len(shape))
    return pl.pallas_call(
        functools.partial(_mixer_out_kernel, tiles_per_batch=tiles_per_batch, n_ctx=n_ctx),
        grid=(t // tm,),
        in_specs=[row(D_MODEL, 0), row(D_INNER, 0), row(D_INNER, 0), row(D_INNER, 0),
                  row(D_MODEL, 5), row(D_MODEL, 6), row(D_MODEL, 7), row(D_MODEL, 8), row(D_MODEL, 9),
                  pl.BlockSpec((None, None, 6, D_MODEL), mod_index),
                  const((1, D_INNER)), const((1, D_INNER)), const((D_INNER, D_MODEL)),
                  const((SC_CONV, D_MODEL)), const((D_MODEL, D_MODEL)), const((D_MODEL, D_MODEL))],
        out_specs=row(D_MODEL, 0),
        out_shape=jax.ShapeDtypeStruct((t, D_MODEL), F32),
        compiler_params=_params("parallel"),
        name="mixer_out",
    )(h, y, xbc, p, p, p, p, p, p, mods, d_skip, gain, w_ssd, sc_w, w_sc, w_o)


def _router_kernel(h_ref, mod_ref, gain_ref, wr_ref, br_ref, u_ref, idx_ref, wts_ref):
    u = _modulated_norm(h_ref[...], gain_ref[...], mod_ref[3:4, :], mod_ref[4:5, :])
    _to_slabs(u_ref, u)
    logits = lax.dot_general(wr_ref[...], u, (((1,), (1,)), ((), ())), precision=HIGHEST,
                             preferred_element_type=F32)
    scores = jax.nn.sigmoid(logits)
    sel = scores + br_ref[...]
    best_val = None
    best_group = None
    for g in range(N_EXPERTS // EXPERTS_PER_GROUP):
        v = [sel[EXPERTS_PER_GROUP * g + i:EXPERTS_PER_GROUP * g + i + 1, :] for i in range(EXPERTS_PER_GROUP)]
        top2 = None
        for i in range(EXPERTS_PER_GROUP):
            for j in range(i + 1, EXPERTS_PER_GROUP):
                pair = v[i] + v[j]
                top2 = pair if top2 is None else jnp.maximum(top2, pair)
        if g == 0:
            best_val, best_group = top2, jnp.zeros_like(top2, dtype=jnp.int32)
        else:
            better = top2 > best_val
            best_group = jnp.where(better, g, best_group)
            best_val = jnp.where(better, top2, best_val)
    eidx = lax.broadcasted_iota(jnp.int32, sel.shape, 0)
    masked = jnp.where(eidx // EXPERTS_PER_GROUP == best_group, sel, -jnp.inf)
    m1 = jnp.max(masked, axis=0, keepdims=True)
    i1 = jnp.min(jnp.where(masked == m1, eidx, N_EXPERTS), axis=0, keepdims=True)
    masked2 = jnp.where(eidx == i1, -jnp.inf, masked)
    m2 = jnp.max(masked2, axis=0, keepdims=True)
    i2 = jnp.min(jnp.where(masked2 == m2, eidx, N_EXPERTS), axis=0, keepdims=True)
    w1 = jnp.sum(jnp.where(eidx == i1, scores, 0.0), axis=0, keepdims=True)
    w2 = jnp.sum(jnp.where(eidx == i2, scores, 0.0), axis=0, keepdims=True)
    denom = w1 + w2
    slot = lax.broadcasted_iota(jnp.int32, idx_ref.shape, 0)
    idx_ref[...] = jnp.where(slot == 0, i1, jnp.where(slot == 1, i2, 0))
    wts_ref[...] = jnp.where(slot == 0, w1 / denom, jnp.where(slot == 1, w2 / denom, 0.0))


def _router(h, mods, mod_index, gain, wr_t, b_router):
    t = h.shape[0]
    tm = ROW_TILE
    return pl.pallas_call(
        _router_kernel,
        grid=(t // tm,),
        in_specs=[pl.BlockSpec((tm, D_MODEL), lambda i: (i, 0)),
                  pl.BlockSpec((None, None, 6, D_MODEL), mod_index),
                  pl.BlockSpec((1, D_MODEL), lambda i: (0, 0)),
                  pl.BlockSpec((N_EXPERTS, D_MODEL), lambda i: (0, 0)),
                  pl.BlockSpec((N_EXPERTS, 1), lambda i: (0, 0))],
        out_specs=[pl.BlockSpec((tm * SLAB, 128), lambda i: (i, 0)),
                   pl.BlockSpec((8, tm), lambda i: (0, i)),
                   pl.BlockSpec((8, tm), lambda i: (0, i))],
        out_shape=[jax.ShapeDtypeStruct((t * SLAB, 128), F32),
                   jax.ShapeDtypeStruct((8, t), jnp.int32),
                   jax.ShapeDtypeStruct((8, t), F32)],
        compiler_params=_params("parallel"),
        name="router",
    )(h, mods, gain, wr_t, b_router)


SLAB = 8
ISSUE_UNROLL = 8


def _to_slabs(ref, value):
    n = value.shape[0]
    for s in range(SLAB):
        ref[pl.ds(s, n, stride=SLAB), :] = value[:, 128 * s:128 * (s + 1)]


def _from_slabs(ref, base, n):
    return jnp.concatenate([ref[pl.ds(base + s, n, stride=SLAB), :] for s in range(SLAB)], axis=1)


def _gather_tile(i, n, idx_hbm, src_hbm, idx_smem, rows, sem_idx, sem_rows):
    tm = ROW_TILE

    def idx_copy(k):
        return pltpu.make_async_copy(idx_hbm.at[k], idx_smem.at[pl.ds((k % 2) * tm, tm)], sem_idx.at[k % 2])

    def token_copy(k, r, src_row):
        dst = pl.multiple_of(((k % 2) * tm + r) * SLAB, SLAB)
        return pltpu.make_async_copy(src_hbm.at[pl.ds(pl.multiple_of(src_row, SLAB), SLAB)],
                                     rows.at[pl.ds(dst, SLAB)], sem_rows.at[k % 2])

    def issue_rows(k):
        def body(blk, carry):
            for j in range(ISSUE_UNROLL):
                r = blk * ISSUE_UNROLL + j
                token_copy(k, r, idx_smem[(k % 2) * tm + r]).start(priority=j % 2)
            return carry
        lax.fori_loop(0, tm // ISSUE_UNROLL, body, 0)

    def wait_rows(k):
        def body(r, carry):
            token_copy(k, r, 0).wait()
            return carry
        lax.fori_loop(0, tm, body, 0, unroll=8)

    @pl.when((i == 0) & (n > 0))
    def _():
        idx_copy(0).start()
        idx_copy(0).wait()
        issue_rows(0)

    @pl.when((i == 0) & (n > 1))
    def _():
        idx_copy(1).start()

    @pl.when(i + 1 < n)
    def _():
        idx_copy(i + 1).wait()
        issue_rows(i + 1)

    @pl.when(i + 2 < n)
    def _():
        idx_copy(i + 2).start()

    @pl.when(i < n)
    def _():
        wait_rows(i)


def _experts_kernel(lo_ref, hi_ref, n_ref, idx_hbm, u_hbm, wts_ref, wg_lo, wu_lo, wd_lo, wg_hi, wu_hi, wd_hi,
                    y_ref, idx_smem, rows, sem_idx, sem_rows):
    i = pl.program_id(0)
    n = n_ref[0]
    _gather_tile(i, n, idx_hbm, u_hbm, idx_smem, rows, sem_idx, sem_rows)

    @pl.when(i < n)
    def _():
        u = _from_slabs(rows, (i % 2) * (ROW_TILE * SLAB), ROW_TILE).astype(BF16)

        def expert(wg, wu, wd):
            hid = _silu(jnp.dot(u, wg[...], preferred_element_type=F32)) * jnp.dot(u, wu[...], preferred_element_type=F32)
            return jnp.dot(hid.astype(BF16), wd[...], preferred_element_type=F32)

        _to_slabs(y_ref, wts_ref[:, 0:1] * expert(wg_lo, wu_lo, wd_lo)
                  + wts_ref[:, 1:2] * expert(wg_hi, wu_hi, wd_hi))

    @pl.when(i >= n)
    def _():
        y_ref[...] = jnp.zeros_like(y_ref)


def _experts(tile_lo, tile_hi, n_used, src_idx, u, wts_sorted, w_gate, w_up, w_down):
    n_tiles = src_idx.shape[0]
    tm = ROW_TILE
    w_in_spec = lambda which: pl.BlockSpec((None, D_MODEL, D_EXPERT),
                                           lambda i, lo, hi, n: ((lo, hi)[which][i], 0, 0))
    w_out_spec = lambda which: pl.BlockSpec((None, D_EXPERT, D_MODEL),
                                            lambda i, lo, hi, n: ((lo, hi)[which][i], 0, 0))
    return pl.pallas_call(
        _experts_kernel,
        grid_spec=pltpu.PrefetchScalarGridSpec(
            num_scalar_prefetch=3,
            grid=(n_tiles,),
            in_specs=[pl.BlockSpec(memory_space=pl.ANY),
                      pl.BlockSpec(memory_space=pl.ANY),
                      pl.BlockSpec((tm, 2), lambda i, lo, hi, n: (i, 0)),
                      w_in_spec(0), w_in_spec(0), w_out_spec(0),
                      w_in_spec(1), w_in_spec(1), w_out_spec(1)],
            out_specs=pl.BlockSpec((tm * SLAB, 128), lambda i, lo, hi, n: (i, 0)),
            scratch_shapes=[pltpu.SMEM((2 * tm,), jnp.int32),
                            pltpu.VMEM((2 * tm * SLAB, 128), F32),
                            pltpu.SemaphoreType.DMA((2,)),
                            pltpu.SemaphoreType.DMA((2,))]),
        out_shape=jax.ShapeDtypeStruct((n_tiles * tm * SLAB, 128), F32),
        compiler_params=_params("arbitrary"),
        name="experts",
    )(tile_lo, tile_hi, n_used, src_idx, u, wts_sorted, w_gate, w_up, w_down, w_gate, w_up, w_down)


def _combine_kernel(pos_hbm, y_hbm, h_ref, mod_ref, o_ref, idx_smem, rows, sem_idx, sem_rows):
    i = pl.program_id(0)
    _gather_tile(i, pl.num_programs(0), pos_hbm, y_hbm, idx_smem, rows, sem_idx, sem_rows)
    o_ref[...] = h_ref[...] + mod_ref[5:6, :] * _from_slabs(rows, (i % 2) * (ROW_TILE * SLAB), ROW_TILE)


def _combine(pos, y_sorted, h, mods, mod_index):
    t = h.shape[0]
    tm = ROW_TILE
    return pl.pallas_call(
        _combine_kernel,
        grid=(t // tm,),
        in_specs=[pl.BlockSpec(memory_space=pl.ANY),
                  pl.BlockSpec(memory_space=pl.ANY),
                  pl.BlockSpec((tm, D_MODEL), lambda i: (i, 0)),
                  pl.BlockSpec((None, None, 6, D_MODEL), mod_index)],
        out_specs=pl.BlockSpec((tm, D_MODEL), lambda i: (i, 0)),
        out_shape=jax.ShapeDtypeStruct((t, D_MODEL), F32),
        scratch_shapes=[pltpu.SMEM((2 * tm,), jnp.int32),
                        pltpu.VMEM((2 * tm * SLAB, 128), F32),
                        pltpu.SemaphoreType.DMA((2,)),
                        pltpu.SemaphoreType.DMA((2,))],
        compiler_params=_params("arbitrary"),
        name="moe_combine",
    )((pos * SLAB).reshape(t // tm, tm), y_sorted, h, mods)


N_PAIRS = EXPERTS_PER_GROUP * (EXPERTS_PER_GROUP - 1) // 2
N_CLASSES = (N_EXPERTS // EXPERTS_PER_GROUP) * N_PAIRS


def _routing_tables(idx, wts, t):
    tm = ROW_TILE
    n_tiles = t // tm + N_CLASSES
    i1, i2 = idx[0], idx[1]
    lo, hi = jnp.minimum(i1, i2), jnp.maximum(i1, i2)
    first_is_lo = i1 < i2
    w_lo = jnp.where(first_is_lo, wts[0], wts[1])
    w_hi = jnp.where(first_is_lo, wts[1], wts[0])
    pairs = [(a, b) for a in range(EXPERTS_PER_GROUP) for b in range(a + 1, EXPERTS_PER_GROUP)]
    pair_of = [[0] * EXPERTS_PER_GROUP for _ in range(EXPERTS_PER_GROUP)]
    for k, (a, b) in enumerate(pairs):
        pair_of[a][b] = k
    pair_tab = jnp.asarray(pair_of, jnp.int32).reshape(-1)
    cls = (lo // EXPERTS_PER_GROUP) * N_PAIRS + pair_tab[(lo % EXPERTS_PER_GROUP) * EXPERTS_PER_GROUP
                                                         + hi % EXPERTS_PER_GROUP]
    onehot = (cls[:, None] == jnp.arange(N_CLASSES, dtype=jnp.int32)[None, :]).astype(jnp.int32)
    running = jnp.cumsum(onehot, axis=0)
    rank = jnp.sum(running * onehot, axis=1) - 1
    counts = running[-1]
    padded = (counts + tm - 1) // tm * tm
    ends = jnp.cumsum(padded)
    pos = (ends - padded)[cls] + rank
    n_used = (ends[-1] // tm).astype(jnp.int32).reshape(1)
    packed = jnp.stack([jnp.arange(t, dtype=jnp.int32) * SLAB,
                        lax.bitcast_convert_type(w_lo, jnp.int32),
                        lax.bitcast_convert_type(w_hi, jnp.int32)], axis=1)
    packed = jnp.zeros((n_tiles * tm, 3), jnp.int32).at[pos].set(packed)
    src = packed[:, 0]
    wts_sorted = lax.bitcast_convert_type(packed[:, 1:], F32)
    tile_first = jnp.minimum(jnp.arange(n_tiles, dtype=jnp.int32), n_used[0] - 1) * tm
    tile_cls = jnp.minimum(jnp.sum((tile_first[:, None] >= ends[None, :]).astype(jnp.int32), axis=1), N_CLASSES - 1)
    class_lo = jnp.asarray([g * EXPERTS_PER_GROUP + a for g in range(N_EXPERTS // EXPERTS_PER_GROUP)
                            for a, _ in pairs], jnp.int32)
    class_hi = jnp.asarray([g * EXPERTS_PER_GROUP + b for g in range(N_EXPERTS // EXPERTS_PER_GROUP)
                            for _, b in pairs], jnp.int32)
    return class_lo[tile_cls], class_hi[tile_cls], n_used, src.reshape(n_tiles, tm), wts_sorted, pos


def _final_norm_kernel(h_ref, gain_ref, o_ref):
    x = h_ref[...]
    ms = jnp.mean(x * x, axis=-1, keepdims=True)
    o_ref[...] = x * lax.rsqrt(ms + EPS) * gain_ref[...]


def _final_norm(h3, gain, n_ctx):
    b, s_len, _ = h3.shape
    tm = ROW_TILE
    skip = n_ctx // tm
    return pl.pallas_call(
        _final_norm_kernel,
        grid=(b, (s_len - n_ctx) // tm),
        in_specs=[pl.BlockSpec((None, tm, D_MODEL), lambda i, j: (i, skip + j, 0)),
                  pl.BlockSpec((1, D_MODEL), lambda i, j: (0, 0))],
        out_specs=pl.BlockSpec((None, tm, D_MODEL), lambda i, j: (i, j, 0)),
        out_shape=jax.ShapeDtypeStruct((b, s_len - n_ctx, D_MODEL), F32),
        compiler_params=_params("parallel", "parallel"),
        name="final_norm",
    )(h3, gain)


def _group_major(t, b, s_len):
    return t[:, :, :2 * HEADS].reshape(b, s_len, 2, GROUPS, HEADS_PER_GROUP)


def kernel(x, c, ctx, c_ctx, w_ada, b_ada, norm_mix, norm_ffn, w_in, ssd_conv_w, ssd_conv_b, ssd_dt_bias,
           ssd_a_log, ssd_d, ssd_norm, w_ssd_out, sc_conv_w, w_sc_out, w_o, w_router, b_router, w_gate, w_up,
           w_down, final_norm):
    b, seq, _ = x.shape
    n_ctx = ctx.shape[1]
    depth = w_ada.shape[0]
    s_len = n_ctx + seq
    t = b * s_len
    nc = s_len // CHUNK
    assert n_ctx == ROW_TILE and seq % ROW_TILE == 0 and ROW_TILE % GRID_W == 0 and ROW_TILE % CHUNK == 0
    tiles_per_batch = s_len // ROW_TILE

    def mod_index(i):
        return (i // tiles_per_batch, jnp.minimum(i % tiles_per_batch, 1), 0, 0)

    rows = -(-(b + 1) // 8) * 8
    cc = jnp.zeros((rows, D_MODEL), F32).at[:b].set(c).at[b].set(c_ctx)
    ada = _ada_params(cc, w_ada, b_ada)
    lat = ada[:, :b].reshape(depth, b, 1, 6, D_MODEL)
    con = jnp.broadcast_to(ada[:, b].reshape(depth, 1, 1, 6, D_MODEL), lat.shape)
    mods_all = jnp.concatenate([con, lat], axis=2)

    spread_mat = _spread_matrix()
    wr_t = w_router.T
    br = b_router.reshape(N_EXPERTS, 1)
    h = jnp.concatenate([ctx, x], axis=1).reshape(t, D_MODEL)

    for l in range(depth):
        mods = mods_all[l]
        w_l = w_in[l]
        w_main = jnp.concatenate([w_l[:, :D_INNER + CONV_DIM], w_l[:, D_INNER + CONV_DIM + 2 * HEADS:]],
                                 axis=1).astype(BF16)
        w_dt = jnp.pad(w_l[:, D_INNER + CONV_DIM:D_INNER + CONV_DIM + 2 * HEADS],
                       ((0, 0), (0, DT_PAD - 2 * HEADS))).astype(BF16)
        dt_bias = jnp.pad(ssd_dt_bias[l].reshape(1, 2 * HEADS), ((0, 0), (0, DT_PAD - 2 * HEADS)))
        alog = jnp.pad(ssd_a_log[l].reshape(1, 2 * HEADS), ((0, 0), (0, DT_PAD - 2 * HEADS)))

        p, dt = _in_proj(h, mods, norm_mix[l].reshape(1, D_MODEL), w_main, w_dt, dt_bias, mod_index)
        xbc = _ssd_conv(p.reshape(b, s_len, P_COLS), ssd_conv_w[l], ssd_conv_b[l].reshape(1, CONV_DIM), n_ctx)
        dt3 = dt.reshape(b, s_len, DT_PAD)
        a2, arow, ea, dw, eal = _decay_tables(dt3, alog)

        cols = jnp.stack([_group_major(v, b, s_len) for v in (a2, ea, dw)], axis=3)
        colslab = cols.transpose(0, 4, 1, 2, 3, 5).reshape(b, GROUPS, s_len, 48)
        rowslab = _group_major(arow, b, s_len).reshape(b, nc, CHUNK, 2, GROUPS, HEADS_PER_GROUP)
        rowslab = rowslab.transpose(0, 4, 3, 1, 5, 2)
        ealx = eal[:, :, 0, :2 * HEADS].reshape(b, nc, 2, GROUPS, HEADS_PER_GROUP).transpose(0, 3, 1, 2, 4)
        ealx = jnp.repeat(ealx, HEAD_DIM, axis=-1)

        y = _ssd_scan(xbc, colslab, rowslab, ealx, spread_mat, n_ctx)
        h = _mixer_out(h, y.reshape(t, D_INNER), xbc.reshape(t, CONV_DIM), p, mods, mod_index,
                       jnp.repeat(ssd_d[l], HEAD_DIM).reshape(1, D_INNER), ssd_norm[l].reshape(1, D_INNER),
                       w_ssd_out[l].astype(BF16), sc_conv_w[l], w_sc_out[l].astype(BF16), w_o[l].astype(BF16),
                       tiles_per_batch, n_ctx)
        u2, idx, wts = _router(h, mods, mod_index, norm_ffn[l].reshape(1, D_MODEL), wr_t, br)
        tile_lo, tile_hi, n_used, src_idx, wts_sorted, pos = _routing_tables(idx, wts, t)
        y_sorted = _experts(tile_lo, tile_hi, n_used, src_idx, u2, wts_sorted,
                            w_gate[l].astype(BF16), w_up[l].astype(BF16), w_down[l].astype(BF16))
        h = _combine(pos, y_sorted, h, mods, mod_index)

    return _final_norm(h.reshape(b, s_len, D_MODEL), final_norm.reshape(1, D_MODEL), n_ctx)
```

```python
import functools

import jax
import jax.numpy as jnp
import numpy as np
from jax import lax
from jax.experimental import pallas as pl
from jax.experimental.pallas import tpu as pltpu

F32 = jnp.float32
BF16 = jnp.bfloat16
HIGHEST = lax.Precision.HIGHEST

D_MODEL = 1024
D_INNER = 2048
HEAD_DIM = 64
HEADS = 32
GROUPS = 4
HEADS_PER_GROUP = HEADS // GROUPS
GROUP_WIDTH = HEADS_PER_GROUP * HEAD_DIM
STATE = 128
CHUNK = 128
CONV_DIM = D_INNER + 2 * GROUPS * STATE
SSD_CONV = 4
SC_CONV = 3
GRID_W = 64
N_EXPERTS = 16
EXPERTS_PER_GROUP = 4
D_EXPERT = 512
EPS = 1e-6
P_COLS = 10 * D_MODEL
DT_PAD = 128
ROW_TILE = 256
VMEM_LIMIT = 56 * 1024 * 1024


def _params(*sem):
    return pltpu.CompilerParams(dimension_semantics=sem, vmem_limit_bytes=VMEM_LIMIT)


def _silu(v):
    return v * jax.nn.sigmoid(v)


def _modulated_norm(x, gain, shift, scale):
    ms = jnp.mean(x * x, axis=-1, keepdims=True)
    return x * lax.rsqrt(ms + EPS) * gain * (1.0 + scale) + shift


def _ada_kernel(c_ref, w_ref, b_ref, o_ref):
    cond = _silu(c_ref[...])
    o_ref[...] = jnp.dot(cond, w_ref[...], precision=HIGHEST, preferred_element_type=F32) + b_ref[...]


def _ada_params(cc, w_ada, b_ada):
    depth, _, n = w_ada.shape
    rows = cc.shape[0]
    tn = 512
    return pl.pallas_call(
        _ada_kernel,
        grid=(depth, n // tn),
        in_specs=[pl.BlockSpec((rows, D_MODEL), lambda l, j: (0, 0)),
                  pl.BlockSpec((None, D_MODEL, tn), lambda l, j: (l, 0, j)),
                  pl.BlockSpec((None, 1, tn), lambda l, j: (l, 0, j))],
        out_specs=pl.BlockSpec((None, rows, tn), lambda l, j: (l, 0, j)),
        out_shape=jax.ShapeDtypeStruct((depth, rows, n), F32),
        compiler_params=_params("parallel", "parallel"),
        name="ada_params",
    )(cc, w_ada, b_ada.reshape(depth, 1, n))


def _in_proj_kernel(h_ref, mod_ref, gain_ref, w_ref, wdt_ref, dtb_ref, p_ref, dt_ref):
    u = _modulated_norm(h_ref[...], gain_ref[...], mod_ref[0:1, :], mod_ref[1:2, :]).astype(BF16)
    for n0 in range(0, P_COLS, 512):
        p_ref[:, n0:n0 + 512] = jnp.dot(u, w_ref[:, n0:n0 + 512], preferred_element_type=F32).astype(BF16)
    raw = jnp.dot(u, wdt_ref[...], preferred_element_type=F32) + dtb_ref[...]
    dt_ref[...] = jnp.maximum(raw, 0.0) + jnp.log(1.0 + jnp.exp(-jnp.abs(raw)))


def _in_proj(h, mods, gain, w_main, w_dt, dt_bias, mod_index):
    t = h.shape[0]
    return pl.pallas_call(
        _in_proj_kernel,
        grid=(t // ROW_TILE,),
        in_specs=[pl.BlockSpec((ROW_TILE, D_MODEL), lambda i: (i, 0)),
                  pl.BlockSpec((None, None, 6, D_MODEL), mod_index),
                  pl.BlockSpec((1, D_MODEL), lambda i: (0, 0)),
                  pl.BlockSpec((D_MODEL, P_COLS), lambda i: (0, 0), pipeline_mode=pl.Buffered(1)),
                  pl.BlockSpec((D_MODEL, DT_PAD), lambda i: (0, 0)),
                  pl.BlockSpec((1, DT_PAD), lambda i: (0, 0))],
        out_specs=[pl.BlockSpec((ROW_TILE, P_COLS), lambda i: (i, 0)),
                   pl.BlockSpec((ROW_TILE, DT_PAD), lambda i: (i, 0))],
        out_shape=[jax.ShapeDtypeStruct((t, P_COLS), BF16),
                   jax.ShapeDtypeStruct((t, DT_PAD), F32)],
        compiler_params=_params("parallel"),
        name="in_proj",
    )(h, mods, gain, w_main, w_dt, dt_bias)


CONV_TAPS = (-1, 1, 2)
CONV_WINDOW = 2 * CHUNK


def _conv_shift_plan(s_len, n_ctx):
    mats, plan, seen = [], [], {}
    for c in range(s_len // CHUNK):
        r0 = c * CHUNK
        start = min(max(r0 - CHUNK // 2, 0), s_len - CONV_WINDOW)
        sel = np.zeros((len(CONV_TAPS) * CHUNK, CONV_WINDOW), np.float32)
        for ti, off in enumerate(CONV_TAPS):
            for l in range(CHUNK):
                dst, src = r0 + l, r0 + l + off
                if 0 <= src < s_len and (src >= n_ctx) == (dst >= n_ctx):
                    sel[ti * CHUNK + l, src - start] = 1.0
        key = sel.tobytes()
        if key not in seen:
            seen[key] = len(mats)
            mats.append(sel)
        plan.append((start, seen[key]))
    return jnp.asarray(np.stack(mats), BF16), tuple(plan)


def _ssd_conv_kernel(x_ref, sel_ref, w_ref, b_ref, o_ref, *, plan):
    for c, (start, kind) in enumerate(plan):
        rows = slice(c * CHUNK, (c + 1) * CHUNK)
        window = x_ref[start:start + CONV_WINDOW, :]
        acc = x_ref[rows, :].astype(F32) * w_ref[1:2, :] + b_ref[...]
        for ti, k in enumerate((0, 2, 3)):
            sh = jnp.dot(sel_ref[kind, ti * CHUNK:(ti + 1) * CHUNK, :], window, preferred_element_type=F32)
            acc = acc + sh * w_ref[k:k + 1, :]
        o_ref[rows, :] = _silu(acc.astype(BF16))


def _ssd_conv(p3, conv_w, conv_b, n_ctx):
    b, s_len, _ = p3.shape
    tc = 256
    first = D_INNER // tc
    sel, plan = _conv_shift_plan(s_len, n_ctx)
    return pl.pallas_call(
        functools.partial(_ssd_conv_kernel, plan=plan),
        grid=(b, CONV_DIM // tc),
        in_specs=[pl.BlockSpec((None, s_len, tc), lambda i, c: (i, 0, first + c)),
                  pl.BlockSpec(sel.shape, lambda i, c: (0, 0, 0)),
                  pl.BlockSpec((SSD_CONV, tc), lambda i, c: (0, c)),
                  pl.BlockSpec((1, tc), lambda i, c: (0, c))],
        out_specs=pl.BlockSpec((None, s_len, tc), lambda i, c: (i, 0, c)),
        out_shape=jax.ShapeDtypeStruct((b, s_len, CONV_DIM), BF16),
        compiler_params=_params("parallel", "parallel"),
        name="ssd_conv",
    )(p3, sel, conv_w, conv_b)


LOG2E = 1.4426950408889634


def _decay_kernel(dt_ref, alog_ref, a2_ref, arow_ref, ea_ref, dw_ref, eal_ref, *, n_chunks):
    neg_a = -jnp.exp(alog_ref[...])
    r = lax.broadcasted_iota(jnp.int32, (CHUNK, CHUNK), 0)
    c = lax.broadcasted_iota(jnp.int32, (CHUNK, CHUNK), 1)
    lower = (c <= r).astype(F32)
    upper = (c >= r).astype(F32)
    is_fwd = lax.broadcasted_iota(jnp.int32, (1, DT_PAD), 1) < HEADS

    def body(ci, carry):
        rows = pl.ds(pl.multiple_of(ci * CHUNK, CHUNK), CHUNK)
        dt = dt_ref[rows, :]
        da = dt * neg_a
        fwd = jnp.dot(lower, da, precision=HIGHEST, preferred_element_type=F32)
        bwd = jnp.dot(upper, da, precision=HIGHEST, preferred_element_type=F32)
        a = jnp.where(is_fwd, fwd, bwd)
        total = jnp.where(is_fwd, fwd[CHUNK - 1:CHUNK, :], bwd[0:1, :])
        a2 = a * LOG2E
        a2_ref[rows, :] = a2
        arow_ref[rows, :] = a2 - jnp.log(dt) * LOG2E
        ea_ref[rows, :] = jnp.exp(a)
        dw_ref[rows, :] = dt * jnp.exp(total - a)
        eal_ref[ci] = jnp.exp(total)
        return carry

    lax.fori_loop(0, n_chunks, body, 0)


def _decay_tables(dt3, alog):
    b, s_len, _ = dt3.shape
    nc = s_len // CHUNK
    tile = pl.BlockSpec((None, s_len, DT_PAD), lambda i: (i, 0, 0))
    full = jax.ShapeDtypeStruct((b, s_len, DT_PAD), F32)
    return pl.pallas_call(
        functools.partial(_decay_kernel, n_chunks=nc),
        grid=(b,),
        in_specs=[tile, pl.BlockSpec((1, DT_PAD), lambda i: (0, 0))],
        out_specs=[tile, tile, tile, tile, pl.BlockSpec((None, nc, 1, DT_PAD), lambda i: (i, 0, 0, 0))],
        out_shape=[full, full, full, full, jax.ShapeDtypeStruct((b, nc, 1, DT_PAD), F32)],
        compiler_params=_params("parallel"),
        name="ssd_decay",
    )(dt3, alog)


def _scan_kernel(x_ref, b_ref, c_ref, col_ref, row_ref, eal_ref, exp_ref, y_ref,
                 state_f, state_b, yacc, *, n_chunks, n_ctx_chunks):
    yacc[...] = jnp.zeros_like(yacc)
    state_f[...] = jnp.zeros_like(state_f)
    state_b[...] = jnp.zeros_like(state_b)
    r = lax.broadcasted_iota(jnp.int32, (CHUNK, CHUNK), 0)
    c = lax.broadcasted_iota(jnp.int32, (CHUNK, CHUNK), 1)
    mask_bias = tuple(jnp.where(keep, 0.0, -jnp.inf).astype(BF16) for keep in (r >= c, r <= c))
    low_half = lax.broadcasted_iota(jnp.int32, (1, 2 * HEAD_DIM), 1) < HEAD_DIM
    states = (state_f, state_b)

    def chunk_step(ci, d):
        r0 = pl.multiple_of(ci * CHUNK, CHUNK)
        xc = x_ref[pl.ds(r0, CHUNK), :]
        bc = b_ref[pl.ds(r0, CHUNK), :]
        cc = c_ref[pl.ds(r0, CHUNK), :]
        cols = col_ref[pl.ds(r0, CHUNK), :]
        arow = row_ref[d, ci]
        cb = lax.dot_general(cc, bc, (((1,), (1,)), ((), ())), preferred_element_type=F32).astype(BF16)
        spread = jnp.dot(cols.astype(BF16), exp_ref[d], preferred_element_type=F32)
        ea_x = spread[:, :GROUP_WIDTH]
        dw_x = spread[:, GROUP_WIDTH:]
        st = states[d][...]
        y = jnp.dot(cc, st.astype(BF16), preferred_element_type=F32) * ea_x
        parts = []
        for k in range(HEADS_PER_GROUP // 2):
            lhs = []
            for j in (2 * k, 2 * k + 1):
                col = 24 * d + j
                seg = (cols[:, col:col + 1] - arow[j:j + 1, :]).astype(BF16)
                lhs.append(jnp.exp2(seg + mask_bias[d]) * cb)
            xp = xc[:, 128 * k:128 * (k + 1)]
            zero = jnp.zeros_like(xp)
            rhs = jnp.concatenate([jnp.where(low_half, xp, zero), jnp.where(low_half, zero, xp)], axis=0)
            parts.append(jnp.dot(jnp.concatenate(lhs, axis=1), rhs, preferred_element_type=F32))
        y = y + jnp.concatenate(parts, axis=1)
        yacc[pl.ds(r0, CHUNK), :] += y
        xw = (xc.astype(F32) * dw_x).astype(BF16)
        upd = lax.dot_general(bc, xw, (((0,), (0,)), ((), ())), preferred_element_type=F32)
        states[d][...] = st * eal_ref[ci, pl.ds(d, 1), :] + upd

    def body(i, carry):
        chunk_step(i, 0)
        cb_idx = jnp.where(i < n_ctx_chunks, n_ctx_chunks - 1 - i, n_chunks + n_ctx_chunks - 1 - i)
        chunk_step(cb_idx, 1)
        return carry

    lax.fori_loop(0, n_chunks, body, 0)
    y_ref[...] = yacc[...].astype(BF16)


def _ssd_scan(xbc, colslab, rowslab, ealx, spread_mat, n_ctx):
    b, s_len, _ = xbc.shape
    nc = s_len // CHUNK
    x_blocks = D_INNER // STATE
    return pl.pallas_call(
        functools.partial(_scan_kernel, n_chunks=nc, n_ctx_chunks=n_ctx // CHUNK),
        grid=(b, GROUPS),
        in_specs=[pl.BlockSpec((None, s_len, GROUP_WIDTH), lambda i, g: (i, 0, g)),
                  pl.BlockSpec((None, s_len, STATE), lambda i, g: (i, 0, x_blocks + g)),
                  pl.BlockSpec((None, s_len, STATE), lambda i, g: (i, 0, x_blocks + GROUPS + g)),
                  pl.BlockSpec((None, None, s_len, 48), lambda i, g: (i, g, 0, 0)),
                  pl.BlockSpec((None, None, 2, nc, HEADS_PER_GROUP, CHUNK), lambda i, g: (i, g, 0, 0, 0, 0)),
                  pl.BlockSpec((None, None, nc, 2, GROUP_WIDTH), lambda i, g: (i, g, 0, 0, 0)),
                  pl.BlockSpec((2, 48, 2 * GROUP_WIDTH), lambda i, g: (0, 0, 0))],
        out_specs=pl.BlockSpec((None, s_len, GROUP_WIDTH), lambda i, g: (i, 0, g)),
        out_shape=jax.ShapeDtypeStruct((b, s_len, D_INNER), BF16),
        scratch_shapes=[pltpu.VMEM((STATE, GROUP_WIDTH), F32),
                        pltpu.VMEM((STATE, GROUP_WIDTH), F32),
                        pltpu.VMEM((s_len, GROUP_WIDTH), F32)],
        compiler_params=_params("parallel", "parallel"),
        name="ssd_scan",
    )(xbc, xbc, xbc, colslab, rowslab, ealx, spread_mat)


def _spread_matrix():
    rows = jnp.arange(48)[:, None]
    lanes = jnp.arange(2 * GROUP_WIDTH)[None, :]
    mats = []
    for d in range(2):
        kind = (rows - 24 * d) // HEADS_PER_GROUP
        head = (rows - 24 * d) % HEADS_PER_GROUP
        valid = (rows >= 24 * d + 8) & (rows < 24 * d + 24)
        hit = valid & (lanes // GROUP_WIDTH == kind - 1) & ((lanes % GROUP_WIDTH) // HEAD_DIM == head)
        mats.append(hit.astype(BF16))
    return jnp.stack(mats)


def _mixer_out_kernel(h_ref, y_ref, x_ref, z_ref, scb_ref, scc_ref, sch_ref, ga_ref, gb_ref, mod_ref,
                      dskip_ref, gain_ref, wssd_ref, scw_ref, wsc_ref, wo_ref, o_ref,
                      *, tile_of, tiles_per_batch, n_ctx):
    v = scc_ref[...].astype(F32) * sch_ref[...].astype(F32)
    tm = v.shape[0]
    pos = (tile_of(pl.program_id(0)) % tiles_per_batch) * tm + lax.broadcasted_iota(jnp.int32, (tm, 1), 0)
    in_ctx = pos < n_ctx
    col = (pos - n_ctx) % GRID_W
    first = jnp.where(in_ctx, pos, col)
    last = jnp.where(in_ctx, n_ctx - 1 - pos, GRID_W - 1 - col)
    left_ok = first != 0
    right_ok = last != 0
    vl = jnp.where(left_ok, pltpu.roll(v, 1, 0), 0.0)
    vr = jnp.where(right_ok, pltpu.roll(v, tm - 1, 0), 0.0)
    cv = vl * scw_ref[0:1, :] + v * scw_ref[1:2, :] + vr * scw_ref[2:3, :]
    gated = (scb_ref[...].astype(F32) * cv).astype(BF16)

    half = tm // 2
    for r0 in (0, half):
        rows = slice(r0, r0 + half)
        yv = y_ref[rows, :].astype(F32) + x_ref[rows, :].astype(F32) * dskip_ref[...]
        yz = yv * _silu(z_ref[rows, :]).astype(F32)
        ms = jnp.mean(yz * yz, axis=-1, keepdims=True)
        yn = (yz * lax.rsqrt(ms + EPS) * gain_ref[...]).astype(BF16)
        ssd = jnp.dot(yn, wssd_ref[...], preferred_element_type=F32)
        sc = jnp.dot(gated[rows, :], wsc_ref[...], preferred_element_type=F32)
        mix = (jax.nn.sigmoid(ga_ref[rows, :]) * ssd.astype(BF16)
               + jax.nn.sigmoid(gb_ref[rows, :]) * sc.astype(BF16))
        out = jnp.dot(mix, wo_ref[...], preferred_element_type=F32)
        o_ref[rows, :] = h_ref[rows, :] + mod_ref[2:3, :] * out


def _mixer_out(h, y, xbc, p, mods, mod_index, d_skip, gain, w_ssd, sc_w, w_sc, w_o, tiles_per_batch, n_ctx,
               latent_only):
    t = h.shape[0]
    tm = ROW_TILE
    skip = n_ctx // tm if latent_only else 0
    per_batch = tiles_per_batch - skip

    def tile_of(i):
        return (i // per_batch) * tiles_per_batch + skip + i % per_batch

    row = lambda width, j: pl.BlockSpec((tm, width), lambda i: (tile_of(i), j))
    const = lambda shape: pl.BlockSpec(shape, lambda i: (0,) * len(shape))
    n_steps = (t // tm) // tiles_per_batch * per_batch
    return pl.pallas_call(
        functools.partial(_mixer_out_kernel, tile_of=tile_of, tiles_per_batch=tiles_per_batch, n_ctx=n_ctx),
        grid=(n_steps,),
        in_specs=[row(D_MODEL, 0), row(D_INNER, 0), row(D_INNER, 0), row(D_INNER, 0),
                  row(D_MODEL, 5), row(D_MODEL, 6), row(D_MODEL, 7), row(D_MODEL, 8), row(D_MODEL, 9),
                  pl.BlockSpec((None, None, 6, D_MODEL), lambda i: mod_index(tile_of(i))),
                  const((1, D_INNER)), const((1, D_INNER)), const((D_INNER, D_MODEL)),
                  const((SC_CONV, D_MODEL)), const((D_MODEL, D_MODEL)), const((D_MODEL, D_MODEL))],
        out_specs=pl.BlockSpec((tm, D_MODEL), lambda i: (i, 0)),
        out_shape=jax.ShapeDtypeStruct((n_steps * tm, D_MODEL), F32),
        compiler_params=_params("parallel"),
        name="mixer_out",
    )(h, y, xbc, p, p, p, p, p, p, mods, d_skip, gain, w_ssd, sc_w, w_sc, w_o)


def _router_kernel(h_ref, mod_ref, gain_ref, wr_ref, br_ref, u_ref, idx_ref, wts_ref):
    u = _modulated_norm(h_ref[...], gain_ref[...], mod_ref[3:4, :], mod_ref[4:5, :])
    _to_slabs(u_ref, u)
    logits = lax.dot_general(wr_ref[...], u, (((1,), (1,)), ((), ())), precision=HIGHEST,
                             preferred_element_type=F32)
    scores = jax.nn.sigmoid(logits)
    sel = scores + br_ref[...]
    best_val = None
    best_group = None
    for g in range(N_EXPERTS // EXPERTS_PER_GROUP):
        v = [sel[EXPERTS_PER_GROUP * g + i:EXPERTS_PER_GROUP * g + i + 1, :] for i in range(EXPERTS_PER_GROUP)]
        top2 = None
        for i in range(EXPERTS_PER_GROUP):
            for j in range(i + 1, EXPERTS_PER_GROUP):
                pair = v[i] + v[j]
                top2 = pair if top2 is None else jnp.maximum(top2, pair)
        if g == 0:
            best_val, best_group = top2, jnp.zeros_like(top2, dtype=jnp.int32)
        else:
            better = top2 > best_val
            best_group = jnp.where(better, g, best_group)
            best_val = jnp.where(better, top2, best_val)
    eidx = lax.broadcasted_iota(jnp.int32, sel.shape, 0)
    masked = jnp.where(eidx // EXPERTS_PER_GROUP == best_group, sel, -jnp.inf)
    m1 = jnp.max(masked, axis=0, keepdims=True)
    i1 = jnp.min(jnp.where(masked == m1, eidx, N_EXPERTS), axis=0, keepdims=True)
    masked2 = jnp.where(eidx == i1, -jnp.inf, masked)
    m2 = jnp.max(masked2, axis=0, keepdims=True)
    i2 = jnp.min(jnp.where(masked2 == m2, eidx, N_EXPERTS), axis=0, keepdims=True)
    w1 = jnp.sum(jnp.where(eidx == i1, scores, 0.0), axis=0, keepdims=True)
    w2 = jnp.sum(jnp.where(eidx == i2, scores, 0.0), axis=0, keepdims=True)
    denom = w1 + w2
    slot = lax.broadcasted_iota(jnp.int32, idx_ref.shape, 0)
    idx_ref[...] = jnp.where(slot == 0, i1, jnp.where(slot == 1, i2, 0))
    wts_ref[...] = jnp.where(slot == 0, w1 / denom, jnp.where(slot == 1, w2 / denom, 0.0))


def _router(h, mods, mod_index, gain, wr_t, b_router):
    t = h.shape[0]
    tm = ROW_TILE
    return pl.pallas_call(
        _router_kernel,
        grid=(t // tm,),
        in_specs=[pl.BlockSpec((tm, D_MODEL), lambda i: (i, 0)),
                  pl.BlockSpec((None, None, 6, D_MODEL), mod_index),
                  pl.BlockSpec((1, D_MODEL), lambda i: (0, 0)),
                  pl.BlockSpec((N_EXPERTS, D_MODEL), lambda i: (0, 0)),
                  pl.BlockSpec((N_EXPERTS, 1), lambda i: (0, 0))],
        out_specs=[pl.BlockSpec((tm * SLAB, 128), lambda i: (i, 0)),
                   pl.BlockSpec((8, tm), lambda i: (0, i)),
                   pl.BlockSpec((8, tm), lambda i: (0, i))],
        out_shape=[jax.ShapeDtypeStruct((t * SLAB, 128), F32),
                   jax.ShapeDtypeStruct((8, t), jnp.int32),
                   jax.ShapeDtypeStruct((8, t), F32)],
        compiler_params=_params("parallel"),
        name="router",
    )(h, mods, gain, wr_t, b_router)


SLAB = 8
ISSUE_UNROLL = 8


def _to_slabs(ref, value):
    n = value.shape[0]
    for s in range(SLAB):
        ref[pl.ds(s, n, stride=SLAB), :] = value[:, 128 * s:128 * (s + 1)]


def _from_slabs(ref, base, n):
    return jnp.concatenate([ref[pl.ds(base + s, n, stride=SLAB), :] for s in range(SLAB)], axis=1)


def _gather_tile(i, n, idx_hbm, src_hbm, idx_smem, rows, sem_idx, sem_rows):
    tm = ROW_TILE

    def idx_copy(k):
        return pltpu.make_async_copy(idx_hbm.at[k], idx_smem.at[pl.ds((k % 2) * tm, tm)], sem_idx.at[k % 2])

    def token_copy(k, r, src_row):
        dst = pl.multiple_of(((k % 2) * tm + r) * SLAB, SLAB)
        return pltpu.make_async_copy(src_hbm.at[pl.ds(pl.multiple_of(src_row, SLAB), SLAB)],
                                     rows.at[pl.ds(dst, SLAB)], sem_rows.at[k % 2])

    def issue_rows(k):
        def body(blk, carry):
            for j in range(ISSUE_UNROLL):
                r = blk * ISSUE_UNROLL + j
                token_copy(k, r, idx_smem[(k % 2) * tm + r]).start(priority=j % 2)
            return carry
        lax.fori_loop(0, tm // ISSUE_UNROLL, body, 0)

    def wait_rows(k):
        half = pl.ds(pl.multiple_of((k % 2) * tm * SLAB, SLAB), tm * SLAB)
        pltpu.make_async_copy(src_hbm.at[pl.ds(0, tm * SLAB)], rows.at[half], sem_rows.at[k % 2]).wait()

    @pl.when((i == 0) & (n > 0))
    def _():
        idx_copy(0).start()
        idx_copy(0).wait()
        issue_rows(0)

    @pl.when((i == 0) & (n > 1))
    def _():
        idx_copy(1).start()

    @pl.when(i + 1 < n)
    def _():
        idx_copy(i + 1).wait()
        issue_rows(i + 1)

    @pl.when(i + 2 < n)
    def _():
        idx_copy(i + 2).start()

    @pl.when(i < n)
    def _():
        wait_rows(i)


def _experts_kernel(lo_ref, hi_ref, n_ref, idx_hbm, u_hbm, wts_ref, wg_lo, wu_lo, wd_lo, wg_hi, wu_hi, wd_hi,
                    y_ref, idx_smem, rows, sem_idx, sem_rows):
    i = pl.program_id(0)
    n = n_ref[0]
    _gather_tile(i, n, idx_hbm, u_hbm, idx_smem, rows, sem_idx, sem_rows)

    @pl.when(i < n)
    def _():
        u = _from_slabs(rows, (i % 2) * (ROW_TILE * SLAB), ROW_TILE).astype(BF16)

        def expert(wg, wu, wd):
            hid = _silu(jnp.dot(u, wg[...], preferred_element_type=F32)) * jnp.dot(u, wu[...], preferred_element_type=F32)
            return jnp.dot(hid.astype(BF16), wd[...], preferred_element_type=F32)

        _to_slabs(y_ref, wts_ref[:, 0:1] * expert(wg_lo, wu_lo, wd_lo)
                  + wts_ref[:, 1:2] * expert(wg_hi, wu_hi, wd_hi))

    @pl.when(i >= n)
    def _():
        y_ref[...] = jnp.zeros_like(y_ref)


def _experts(tile_lo, tile_hi, n_used, src_idx, u, wts_sorted, w_gate, w_up, w_down):
    n_tiles = src_idx.shape[0]
    tm = ROW_TILE
    w_in_spec = lambda which: pl.BlockSpec((None, D_MODEL, D_EXPERT),
                                           lambda i, lo, hi, n: ((lo, hi)[which][i], 0, 0))
    w_out_spec = lambda which: pl.BlockSpec((None, D_EXPERT, D_MODEL),
                                            lambda i, lo, hi, n: ((lo, hi)[which][i], 0, 0))
    return pl.pallas_call(
        _experts_kernel,
        grid_spec=pltpu.PrefetchScalarGridSpec(
            num_scalar_prefetch=3,
            grid=(n_tiles,),
            in_specs=[pl.BlockSpec(memory_space=pl.ANY),
                      pl.BlockSpec(memory_space=pl.ANY),
                      pl.BlockSpec((tm, 2), lambda i, lo, hi, n: (i, 0)),
                      w_in_spec(0), w_in_spec(0), w_out_spec(0),
                      w_in_spec(1), w_in_spec(1), w_out_spec(1)],
            out_specs=pl.BlockSpec((tm * SLAB, 128), lambda i, lo, hi, n: (i, 0)),
            scratch_shapes=[pltpu.SMEM((2 * tm,), jnp.int32),
                            pltpu.VMEM((2 * tm * SLAB, 128), F32),
                            pltpu.SemaphoreType.DMA((2,)),
                            pltpu.SemaphoreType.DMA((2,))]),
        out_shape=jax.ShapeDtypeStruct((n_tiles * tm * SLAB, 128), F32),
        compiler_params=_params("arbitrary"),
        name="experts",
    )(tile_lo, tile_hi, n_used, src_idx, u, wts_sorted, w_gate, w_up, w_down, w_gate, w_up, w_down)


def _combine_kernel(pos_hbm, y_hbm, h_ref, mod_ref, gain_ref, o_ref, idx_smem, rows, sem_idx, sem_rows,
                    *, final_norm):
    i = pl.program_id(0)
    _gather_tile(i, pl.num_programs(0), pos_hbm, y_hbm, idx_smem, rows, sem_idx, sem_rows)
    out = h_ref[...] + mod_ref[5:6, :] * _from_slabs(rows, (i % 2) * (ROW_TILE * SLAB), ROW_TILE)
    if final_norm:
        ms = jnp.mean(out * out, axis=-1, keepdims=True)
        out = out * lax.rsqrt(ms + EPS) * gain_ref[...]
    o_ref[...] = out


def _combine(pos, y_sorted, h, mods, mod_index, gain, final_norm):
    t = h.shape[0]
    tm = ROW_TILE
    return pl.pallas_call(
        functools.partial(_combine_kernel, final_norm=final_norm),
        grid=(t // tm,),
        in_specs=[pl.BlockSpec(memory_space=pl.ANY),
                  pl.BlockSpec(memory_space=pl.ANY),
                  pl.BlockSpec((tm, D_MODEL), lambda i: (i, 0)),
                  pl.BlockSpec((None, None, 6, D_MODEL), mod_index),
                  pl.BlockSpec((1, D_MODEL), lambda i: (0, 0))],
        out_specs=pl.BlockSpec((tm, D_MODEL), lambda i: (i, 0)),
        out_shape=jax.ShapeDtypeStruct((t, D_MODEL), F32),
        scratch_shapes=[pltpu.SMEM((2 * tm,), jnp.int32),
                        pltpu.VMEM((2 * tm * SLAB, 128), F32),
                        pltpu.SemaphoreType.DMA((2,)),
                        pltpu.SemaphoreType.DMA((2,))],
        compiler_params=_params("arbitrary"),
        name="moe_combine",
    )((pos * SLAB).reshape(t // tm, tm), y_sorted, h, mods, gain)


N_PAIRS = EXPERTS_PER_GROUP * (EXPERTS_PER_GROUP - 1) // 2
N_CLASSES = (N_EXPERTS // EXPERTS_PER_GROUP) * N_PAIRS


def _routing_tables(idx, wts, t):
    tm = ROW_TILE
    n_tiles = t // tm + N_CLASSES
    i1, i2 = idx[0], idx[1]
    lo, hi = jnp.minimum(i1, i2), jnp.maximum(i1, i2)
    first_is_lo = i1 < i2
    w_lo = jnp.where(first_is_lo, wts[0], wts[1])
    w_hi = jnp.where(first_is_lo, wts[1], wts[0])
    pairs = [(a, b) for a in range(EXPERTS_PER_GROUP) for b in range(a + 1, EXPERTS_PER_GROUP)]
    pair_of = [[0] * EXPERTS_PER_GROUP for _ in range(EXPERTS_PER_GROUP)]
    for k, (a, b) in enumerate(pairs):
        pair_of[a][b] = k
    pair_tab = jnp.asarray(pair_of, jnp.int32).reshape(-1)
    cls = (lo // EXPERTS_PER_GROUP) * N_PAIRS + pair_tab[(lo % EXPERTS_PER_GROUP) * EXPERTS_PER_GROUP
                                                         + hi % EXPERTS_PER_GROUP]
    onehot = (cls[:, None] == jnp.arange(N_CLASSES, dtype=jnp.int32)[None, :]).astype(jnp.int32)
    running = jnp.cumsum(onehot, axis=0)
    rank = jnp.sum(running * onehot, axis=1) - 1
    counts = running[-1]
    padded = (counts + tm - 1) // tm * tm
    ends = jnp.cumsum(padded)
    pos = (ends - padded)[cls] + rank
    n_used = (ends[-1] // tm).astype(jnp.int32).reshape(1)
    packed = jnp.stack([jnp.arange(t, dtype=jnp.int32) * SLAB,
                        lax.bitcast_convert_type(w_lo, jnp.int32),
                        lax.bitcast_convert_type(w_hi, jnp.int32)], axis=1)
    packed = jnp.zeros((n_tiles * tm, 3), jnp.int32).at[pos].set(packed)
    src = packed[:, 0]
    wts_sorted = lax.bitcast_convert_type(packed[:, 1:], F32)
    tile_first = jnp.minimum(jnp.arange(n_tiles, dtype=jnp.int32), n_used[0] - 1) * tm
    tile_cls = jnp.minimum(jnp.sum((tile_first[:, None] >= ends[None, :]).astype(jnp.int32), axis=1), N_CLASSES - 1)
    class_lo = jnp.asarray([g * EXPERTS_PER_GROUP + a for g in range(N_EXPERTS // EXPERTS_PER_GROUP)
                            for a, _ in pairs], jnp.int32)
    class_hi = jnp.asarray([g * EXPERTS_PER_GROUP + b for g in range(N_EXPERTS // EXPERTS_PER_GROUP)
                            for _, b in pairs], jnp.int32)
    return class_lo[tile_cls], class_hi[tile_cls], n_used, src.reshape(n_tiles, tm), wts_sorted, pos


def _group_major(t, b, s_len):
    return t[:, :, :2 * HEADS].reshape(b, s_len, 2, GROUPS, HEADS_PER_GROUP)


def kernel(x, c, ctx, c_ctx, w_ada, b_ada, norm_mix, norm_ffn, w_in, ssd_conv_w, ssd_conv_b, ssd_dt_bias,
           ssd_a_log, ssd_d, ssd_norm, w_ssd_out, sc_conv_w, w_sc_out, w_o, w_router, b_router, w_gate, w_up,
           w_down, final_norm):
    b, seq, _ = x.shape
    n_ctx = ctx.shape[1]
    depth = w_ada.shape[0]
    s_len = n_ctx + seq
    t = b * s_len
    nc = s_len // CHUNK
    assert n_ctx == ROW_TILE and seq % ROW_TILE == 0 and ROW_TILE % GRID_W == 0 and ROW_TILE % CHUNK == 0
    tiles_per_batch = s_len // ROW_TILE

    def mod_index(i):
        return (i // tiles_per_batch, jnp.minimum(i % tiles_per_batch, 1), 0, 0)

    rows = -(-(b + 1) // 8) * 8
    cc = jnp.zeros((rows, D_MODEL), F32).at[:b].set(c).at[b].set(c_ctx)
    ada = _ada_params(cc, w_ada, b_ada)
    lat = ada[:, :b].reshape(depth, b, 1, 6, D_MODEL)
    con = jnp.broadcast_to(ada[:, b].reshape(depth, 1, 1, 6, D_MODEL), lat.shape)
    mods_all = jnp.concatenate([con, lat], axis=2)

    spread_mat = _spread_matrix()
    wr_t = w_router.T
    br = b_router.reshape(N_EXPERTS, 1)
    h = jnp.concatenate([ctx, x], axis=1).reshape(t, D_MODEL)

    for l in range(depth):
        mods = mods_all[l]
        w_l = w_in[l]
        w_main = jnp.concatenate([w_l[:, :D_INNER + CONV_DIM], w_l[:, D_INNER + CONV_DIM + 2 * HEADS:]],
                                 axis=1).astype(BF16)
        w_dt = jnp.pad(w_l[:, D_INNER + CONV_DIM:D_INNER + CONV_DIM + 2 * HEADS],
                       ((0, 0), (0, DT_PAD - 2 * HEADS))).astype(BF16)
        dt_bias = jnp.pad(ssd_dt_bias[l].reshape(1, 2 * HEADS), ((0, 0), (0, DT_PAD - 2 * HEADS)))
        alog = jnp.pad(ssd_a_log[l].reshape(1, 2 * HEADS), ((0, 0), (0, DT_PAD - 2 * HEADS)))

        p, dt = _in_proj(h, mods, norm_mix[l].reshape(1, D_MODEL), w_main, w_dt, dt_bias, mod_index)
        xbc = _ssd_conv(p.reshape(b, s_len, P_COLS), ssd_conv_w[l], ssd_conv_b[l].reshape(1, CONV_DIM), n_ctx)
        dt3 = dt.reshape(b, s_len, DT_PAD)
        a2, arow, ea, dw, eal = _decay_tables(dt3, alog)

        cols = jnp.stack([_group_major(v, b, s_len) for v in (a2, ea, dw)], axis=3)
        colslab = cols.transpose(0, 4, 1, 2, 3, 5).reshape(b, GROUPS, s_len, 48)
        rowslab = _group_major(arow, b, s_len).reshape(b, nc, CHUNK, 2, GROUPS, HEADS_PER_GROUP)
        rowslab = rowslab.transpose(0, 4, 3, 1, 5, 2)
        ealx = eal[:, :, 0, :2 * HEADS].reshape(b, nc, 2, GROUPS, HEADS_PER_GROUP).transpose(0, 3, 1, 2, 4)
        ealx = jnp.repeat(ealx, HEAD_DIM, axis=-1)

        y = _ssd_scan(xbc, colslab, rowslab, ealx, spread_mat, n_ctx)
        last = l == depth - 1
        h = _mixer_out(h, y.reshape(t, D_INNER), xbc.reshape(t, CONV_DIM), p, mods, mod_index,
                       jnp.repeat(ssd_d[l], HEAD_DIM).reshape(1, D_INNER), ssd_norm[l].reshape(1, D_INNER),
                       w_ssd_out[l].astype(BF16), sc_conv_w[l], w_sc_out[l].astype(BF16), w_o[l].astype(BF16),
                       tiles_per_batch, n_ctx, latent_only=last)
        ffn_index = (lambda i: (i // (seq // ROW_TILE), 1, 0, 0)) if last else mod_index
        t_ffn = h.shape[0]
        u2, idx, wts = _router(h, mods, ffn_index, norm_ffn[l].reshape(1, D_MODEL), wr_t, br)
        tile_lo, tile_hi, n_used, src_idx, wts_sorted, pos = _routing_tables(idx, wts, t_ffn)
        y_sorted = _experts(tile_lo, tile_hi, n_used, src_idx, u2, wts_sorted,
                            w_gate[l].astype(BF16), w_up[l].astype(BF16), w_down[l].astype(BF16))
        h = _combine(pos, y_sorted, h, mods, ffn_index, final_norm.reshape(1, D_MODEL), final_norm=last)

    return h.reshape(b, seq, D_MODEL)
```

```python
import functools

import jax
import jax.numpy as jnp
import numpy as np
from jax import lax
from jax.experimental import pallas as pl
from jax.experimental.pallas import tpu as pltpu

F32 = jnp.float32
BF16 = jnp.bfloat16
HIGHEST = lax.Precision.HIGHEST

D_MODEL = 1024
D_INNER = 2048
HEAD_DIM = 64
HEADS = 32
GROUPS = 4
HEADS_PER_GROUP = HEADS // GROUPS
GROUP_WIDTH = HEADS_PER_GROUP * HEAD_DIM
STATE = 128
CHUNK = 128
CONV_DIM = D_INNER + 2 * GROUPS * STATE
SSD_CONV = 4
SC_CONV = 3
GRID_W = 64
N_EXPERTS = 16
EXPERTS_PER_GROUP = 4
D_EXPERT = 512
EPS = 1e-6
P_COLS = 10 * D_MODEL
DT_PAD = 128
ROW_TILE = 256
VMEM_LIMIT = 56 * 1024 * 1024


def _params(*sem):
    return pltpu.CompilerParams(dimension_semantics=sem, vmem_limit_bytes=VMEM_LIMIT)


def _silu(v):
    return v * jax.nn.sigmoid(v)


def _modulated_norm(x, gain, shift, scale):
    ms = jnp.mean(x * x, axis=-1, keepdims=True)
    return x * lax.rsqrt(ms + EPS) * gain * (1.0 + scale) + shift


def _ada_kernel(c_ref, w_ref, b_ref, o_ref):
    cond = _silu(c_ref[...])
    o_ref[...] = jnp.dot(cond, w_ref[...], precision=HIGHEST, preferred_element_type=F32) + b_ref[...]


def _ada_params(cc, w_ada, b_ada):
    depth, _, n = w_ada.shape
    rows = cc.shape[0]
    tn = 512
    return pl.pallas_call(
        _ada_kernel,
        grid=(depth, n // tn),
        in_specs=[pl.BlockSpec((rows, D_MODEL), lambda l, j: (0, 0)),
                  pl.BlockSpec((None, D_MODEL, tn), lambda l, j: (l, 0, j)),
                  pl.BlockSpec((None, 1, tn), lambda l, j: (l, 0, j))],
        out_specs=pl.BlockSpec((None, rows, tn), lambda l, j: (l, 0, j)),
        out_shape=jax.ShapeDtypeStruct((depth, rows, n), F32),
        compiler_params=_params("parallel", "parallel"),
        name="ada_params",
    )(cc, w_ada, b_ada.reshape(depth, 1, n))


def _in_proj_kernel(h_ref, mod_ref, gain_ref, w_ref, wdt_ref, dtb_ref, p_ref, dt_ref):
    u = _modulated_norm(h_ref[...], gain_ref[...], mod_ref[0:1, :], mod_ref[1:2, :]).astype(BF16)
    for n0 in range(0, P_COLS, 512):
        p_ref[:, n0:n0 + 512] = jnp.dot(u, w_ref[:, n0:n0 + 512], preferred_element_type=F32).astype(BF16)
    raw = jnp.dot(u, wdt_ref[...], preferred_element_type=F32) + dtb_ref[...]
    dt_ref[...] = jnp.maximum(raw, 0.0) + jnp.log(1.0 + jnp.exp(-jnp.abs(raw)))


def _in_proj(h, mods, gain, w_main, w_dt, dt_bias, mod_index):
    t = h.shape[0]
    return pl.pallas_call(
        _in_proj_kernel,
        grid=(t // ROW_TILE,),
        in_specs=[pl.BlockSpec((ROW_TILE, D_MODEL), lambda i: (i, 0)),
                  pl.BlockSpec((None, None, 6, D_MODEL), mod_index),
                  pl.BlockSpec((1, D_MODEL), lambda i: (0, 0)),
                  pl.BlockSpec((D_MODEL, P_COLS), lambda i: (0, 0), pipeline_mode=pl.Buffered(1)),
                  pl.BlockSpec((D_MODEL, DT_PAD), lambda i: (0, 0)),
                  pl.BlockSpec((1, DT_PAD), lambda i: (0, 0))],
        out_specs=[pl.BlockSpec((ROW_TILE, P_COLS), lambda i: (i, 0)),
                   pl.BlockSpec((ROW_TILE, DT_PAD), lambda i: (i, 0))],
        out_shape=[jax.ShapeDtypeStruct((t, P_COLS), BF16),
                   jax.ShapeDtypeStruct((t, DT_PAD), F32)],
        compiler_params=_params("parallel"),
        name="in_proj",
    )(h, mods, gain, w_main, w_dt, dt_bias)


CONV_TAPS = (-1, 1, 2)
CONV_WINDOW = 2 * CHUNK


def _conv_shift_plan(s_len, n_ctx):
    mats, plan, seen = [], [], {}
    for c in range(s_len // CHUNK):
        r0 = c * CHUNK
        start = min(max(r0 - CHUNK // 2, 0), s_len - CONV_WINDOW)
        sel = np.zeros((len(CONV_TAPS) * CHUNK, CONV_WINDOW), np.float32)
        for ti, off in enumerate(CONV_TAPS):
            for l in range(CHUNK):
                dst, src = r0 + l, r0 + l + off
                if 0 <= src < s_len and (src >= n_ctx) == (dst >= n_ctx):
                    sel[ti * CHUNK + l, src - start] = 1.0
        key = sel.tobytes()
        if key not in seen:
            seen[key] = len(mats)
            mats.append(sel)
        plan.append((start, seen[key]))
    return jnp.asarray(np.stack(mats), BF16), tuple(plan)


def _ssd_conv_kernel(x_ref, sel_ref, w_ref, b_ref, o_ref, *, plan):
    for c, (start, kind) in enumerate(plan):
        rows = slice(c * CHUNK, (c + 1) * CHUNK)
        window = x_ref[start:start + CONV_WINDOW, :]
        acc = x_ref[rows, :].astype(F32) * w_ref[1:2, :] + b_ref[...]
        for ti, k in enumerate((0, 2, 3)):
            sh = jnp.dot(sel_ref[kind, ti * CHUNK:(ti + 1) * CHUNK, :], window, preferred_element_type=F32)
            acc = acc + sh * w_ref[k:k + 1, :]
        o_ref[rows, :] = _silu(acc.astype(BF16))


def _ssd_conv(p3, conv_w, conv_b, n_ctx):
    b, s_len, _ = p3.shape
    tc = 256
    first = D_INNER // tc
    sel, plan = _conv_shift_plan(s_len, n_ctx)
    return pl.pallas_call(
        functools.partial(_ssd_conv_kernel, plan=plan),
        grid=(b, CONV_DIM // tc),
        in_specs=[pl.BlockSpec((None, s_len, tc), lambda i, c: (i, 0, first + c)),
                  pl.BlockSpec(sel.shape, lambda i, c: (0, 0, 0)),
                  pl.BlockSpec((SSD_CONV, tc), lambda i, c: (0, c)),
                  pl.BlockSpec((1, tc), lambda i, c: (0, c))],
        out_specs=pl.BlockSpec((None, s_len, tc), lambda i, c: (i, 0, c)),
        out_shape=jax.ShapeDtypeStruct((b, s_len, CONV_DIM), BF16),
        compiler_params=_params("parallel", "parallel"),
        name="ssd_conv",
    )(p3, sel, conv_w, conv_b)


LOG2E = 1.4426950408889634


def _decay_kernel(dt_ref, alog_ref, a2_ref, arow_ref, ea_ref, dw_ref, eal_ref, *, n_chunks):
    neg_a = -jnp.exp(alog_ref[...])
    r = lax.broadcasted_iota(jnp.int32, (CHUNK, CHUNK), 0)
    c = lax.broadcasted_iota(jnp.int32, (CHUNK, CHUNK), 1)
    lower = (c <= r).astype(F32)
    upper = (c >= r).astype(F32)
    is_fwd = lax.broadcasted_iota(jnp.int32, (1, DT_PAD), 1) < HEADS

    def body(ci, carry):
        rows = pl.ds(pl.multiple_of(ci * CHUNK, CHUNK), CHUNK)
        dt = dt_ref[rows, :]
        da = dt * neg_a
        fwd = jnp.dot(lower, da, precision=HIGHEST, preferred_element_type=F32)
        bwd = jnp.dot(upper, da, precision=HIGHEST, preferred_element_type=F32)
        a = jnp.where(is_fwd, fwd, bwd)
        total = jnp.where(is_fwd, fwd[CHUNK - 1:CHUNK, :], bwd[0:1, :])
        a2 = a * LOG2E
        a2_ref[rows, :] = a2
        arow_ref[rows, :] = a2 - jnp.log(dt) * LOG2E
        ea_ref[rows, :] = jnp.exp(a)
        dw_ref[rows, :] = dt * jnp.exp(total - a)
        eal_ref[ci] = jnp.exp(total)
        return carry

    lax.fori_loop(0, n_chunks, body, 0)


def _decay_tables(dt3, alog):
    b, s_len, _ = dt3.shape
    nc = s_len // CHUNK
    tile = pl.BlockSpec((None, s_len, DT_PAD), lambda i: (i, 0, 0))
    full = jax.ShapeDtypeStruct((b, s_len, DT_PAD), F32)
    return pl.pallas_call(
        functools.partial(_decay_kernel, n_chunks=nc),
        grid=(b,),
        in_specs=[tile, pl.BlockSpec((1, DT_PAD), lambda i: (0, 0))],
        out_specs=[tile, tile, tile, tile, pl.BlockSpec((None, nc, 1, DT_PAD), lambda i: (i, 0, 0, 0))],
        out_shape=[full, full, full, full, jax.ShapeDtypeStruct((b, nc, 1, DT_PAD), F32)],
        compiler_params=_params("parallel"),
        name="ssd_decay",
    )(dt3, alog)


def _scan_kernel(x_ref, b_ref, c_ref, col_ref, row_ref, eal_ref, exp_ref, y_ref,
                 state_f, state_b, yacc, *, n_chunks, n_ctx_chunks):
    yacc[...] = jnp.zeros_like(yacc)
    state_f[...] = jnp.zeros_like(state_f)
    state_b[...] = jnp.zeros_like(state_b)
    r = lax.broadcasted_iota(jnp.int32, (CHUNK, CHUNK), 0)
    c = lax.broadcasted_iota(jnp.int32, (CHUNK, CHUNK), 1)
    mask_bias = tuple(jnp.where(keep, 0.0, -jnp.inf).astype(BF16) for keep in (r >= c, r <= c))
    low_half = lax.broadcasted_iota(jnp.int32, (1, 2 * HEAD_DIM), 1) < HEAD_DIM
    states = (state_f, state_b)

    def chunk_step(ci, d):
        r0 = pl.multiple_of(ci * CHUNK, CHUNK)
        xc = x_ref[pl.ds(r0, CHUNK), :]
        bc = b_ref[pl.ds(r0, CHUNK), :]
        cc = c_ref[pl.ds(r0, CHUNK), :]
        cols = col_ref[pl.ds(r0, CHUNK), :]
        arow = row_ref[d, ci]
        cb = lax.dot_general(cc, bc, (((1,), (1,)), ((), ())), preferred_element_type=F32).astype(BF16)
        spread = jnp.dot(cols.astype(BF16), exp_ref[d], preferred_element_type=F32)
        ea_x = spread[:, :GROUP_WIDTH]
        dw_x = spread[:, GROUP_WIDTH:]
        st = states[d][...]
        y = jnp.dot(cc, st.astype(BF16), preferred_element_type=F32) * ea_x
        parts = []
        for k in range(HEADS_PER_GROUP // 2):
            lhs = []
            for j in (2 * k, 2 * k + 1):
                col = 24 * d + j
                seg = (cols[:, col:col + 1] - arow[j:j + 1, :]).astype(BF16)
                lhs.append(jnp.exp2(seg + mask_bias[d]) * cb)
            xp = xc[:, 128 * k:128 * (k + 1)]
            zero = jnp.zeros_like(xp)
            rhs = jnp.concatenate([jnp.where(low_half, xp, zero), jnp.where(low_half, zero, xp)], axis=0)
            parts.append(jnp.dot(jnp.concatenate(lhs, axis=1), rhs, preferred_element_type=F32))
        y = y + jnp.concatenate(parts, axis=1)
        yacc[pl.ds(r0, CHUNK), :] += y
        xw = (xc.astype(F32) * dw_x).astype(BF16)
        upd = lax.dot_general(bc, xw, (((0,), (0,)), ((), ())), preferred_element_type=F32)
        states[d][...] = st * eal_ref[ci, pl.ds(d, 1), :] + upd

    def body(i, carry):
        chunk_step(i, 0)
        cb_idx = jnp.where(i < n_ctx_chunks, n_ctx_chunks - 1 - i, n_chunks + n_ctx_chunks - 1 - i)
        chunk_step(cb_idx, 1)
        return carry

    lax.fori_loop(0, n_chunks, body, 0)
    y_ref[...] = yacc[...].astype(BF16)


def _ssd_scan(xbc, colslab, rowslab, ealx, spread_mat, n_ctx):
    b, s_len, _ = xbc.shape
    nc = s_len // CHUNK
    x_blocks = D_INNER // STATE
    return pl.pallas_call(
        functools.partial(_scan_kernel, n_chunks=nc, n_ctx_chunks=n_ctx // CHUNK),
        grid=(b, GROUPS),
        in_specs=[pl.BlockSpec((None, s_len, GROUP_WIDTH), lambda i, g: (i, 0, g)),
                  pl.BlockSpec((None, s_len, STATE), lambda i, g: (i, 0, x_blocks + g)),
                  pl.BlockSpec((None, s_len, STATE), lambda i, g: (i, 0, x_blocks + GROUPS + g)),
                  pl.BlockSpec((None, None, s_len, 48), lambda i, g: (i, g, 0, 0)),
                  pl.BlockSpec((None, None, 2, nc, HEADS_PER_GROUP, CHUNK), lambda i, g: (i, g, 0, 0, 0, 0)),
                  pl.BlockSpec((None, None, nc, 2, GROUP_WIDTH), lambda i, g: (i, g, 0, 0, 0)),
                  pl.BlockSpec((2, 48, 2 * GROUP_WIDTH), lambda i, g: (0, 0, 0))],
        out_specs=pl.BlockSpec((None, s_len, GROUP_WIDTH), lambda i, g: (i, 0, g)),
        out_shape=jax.ShapeDtypeStruct((b, s_len, D_INNER), BF16),
        scratch_shapes=[pltpu.VMEM((STATE, GROUP_WIDTH), F32),
                        pltpu.VMEM((STATE, GROUP_WIDTH), F32),
                        pltpu.VMEM((s_len, GROUP_WIDTH), F32)],
        compiler_params=_params("parallel", "parallel"),
        name="ssd_scan",
    )(xbc, xbc, xbc, colslab, rowslab, ealx, spread_mat)


def _spread_matrix():
    rows = jnp.arange(48)[:, None]
    lanes = jnp.arange(2 * GROUP_WIDTH)[None, :]
    mats = []
    for d in range(2):
        kind = (rows - 24 * d) // HEADS_PER_GROUP
        head = (rows - 24 * d) % HEADS_PER_GROUP
        valid = (rows >= 24 * d + 8) & (rows < 24 * d + 24)
        hit = valid & (lanes // GROUP_WIDTH == kind - 1) & ((lanes % GROUP_WIDTH) // HEAD_DIM == head)
        mats.append(hit.astype(BF16))
    return jnp.stack(mats)


def _mixer_out_kernel(h_ref, y_ref, x_ref, z_ref, scb_ref, scc_ref, sch_ref, ga_ref, gb_ref, mod_ref,
                      dskip_ref, gain_ref, wssd_ref, scw_ref, wsc_ref, wo_ref, o_ref,
                      *, tile_of, tiles_per_batch, n_ctx):
    v = scc_ref[...].astype(F32) * sch_ref[...].astype(F32)
    tm = v.shape[0]
    pos = (tile_of(pl.program_id(0)) % tiles_per_batch) * tm + lax.broadcasted_iota(jnp.int32, (tm, 1), 0)
    in_ctx = pos < n_ctx
    col = (pos - n_ctx) % GRID_W
    first = jnp.where(in_ctx, pos, col)
    last = jnp.where(in_ctx, n_ctx - 1 - pos, GRID_W - 1 - col)
    left_ok = first != 0
    right_ok = last != 0
    vl = jnp.where(left_ok, pltpu.roll(v, 1, 0), 0.0)
    vr = jnp.where(right_ok, pltpu.roll(v, tm - 1, 0), 0.0)
    cv = vl * scw_ref[0:1, :] + v * scw_ref[1:2, :] + vr * scw_ref[2:3, :]
    gated = (scb_ref[...].astype(F32) * cv).astype(BF16)

    half = tm // 2
    for r0 in (0, half):
        rows = slice(r0, r0 + half)
        yv = y_ref[rows, :].astype(F32) + x_ref[rows, :].astype(F32) * dskip_ref[...]
        yz = yv * _silu(z_ref[rows, :]).astype(F32)
        ms = jnp.mean(yz * yz, axis=-1, keepdims=True)
        yn = (yz * lax.rsqrt(ms + EPS) * gain_ref[...]).astype(BF16)
        ssd = jnp.dot(yn, wssd_ref[...], preferred_element_type=F32)
        sc = jnp.dot(gated[rows, :], wsc_ref[...], preferred_element_type=F32)
        mix = (jax.nn.sigmoid(ga_ref[rows, :]) * ssd.astype(BF16)
               + jax.nn.sigmoid(gb_ref[rows, :]) * sc.astype(BF16))
        out = jnp.dot(mix, wo_ref[...], preferred_element_type=F32)
        o_ref[rows, :] = h_ref[rows, :] + mod_ref[2:3, :] * out


def _mixer_out(h, y, xbc, p, mods, mod_index, d_skip, gain, w_ssd, sc_w, w_sc, w_o, tiles_per_batch, n_ctx,
               latent_only):
    t = h.shape[0]
    tm = ROW_TILE
    skip = n_ctx // tm if latent_only else 0
    per_batch = tiles_per_batch - skip

    def tile_of(i):
        return (i // per_batch) * tiles_per_batch + skip + i % per_batch

    row = lambda width, j: pl.BlockSpec((tm, width), lambda i: (tile_of(i), j))
    const = lambda shape: pl.BlockSpec(shape, lambda i: (0,) * len(shape))
    n_steps = (t // tm) // tiles_per_batch * per_batch
    return pl.pallas_call(
        functools.partial(_mixer_out_kernel, tile_of=tile_of, tiles_per_batch=tiles_per_batch, n_ctx=n_ctx),
        grid=(n_steps,),
        in_specs=[row(D_MODEL, 0), row(D_INNER, 0), row(D_INNER, 0), row(D_INNER, 0),
                  row(D_MODEL, 5), row(D_MODEL, 6), row(D_MODEL, 7), row(D_MODEL, 8), row(D_MODEL, 9),
                  pl.BlockSpec((None, None, 6, D_MODEL), lambda i: mod_index(tile_of(i))),
                  const((1, D_INNER)), const((1, D_INNER)), const((D_INNER, D_MODEL)),
                  const((SC_CONV, D_MODEL)), const((D_MODEL, D_MODEL)), const((D_MODEL, D_MODEL))],
        out_specs=pl.BlockSpec((tm, D_MODEL), lambda i: (i, 0)),
        out_shape=jax.ShapeDtypeStruct((n_steps * tm, D_MODEL), F32),
        compiler_params=_params("parallel"),
        name="mixer_out",
    )(h, y, xbc, p, p, p, p, p, p, mods, d_skip, gain, w_ssd, sc_w, w_sc, w_o)


def _router_kernel(h_ref, mod_ref, gain_ref, wr_ref, br_ref, u_ref, idx_ref, wts_ref):
    u = _modulated_norm(h_ref[...], gain_ref[...], mod_ref[3:4, :], mod_ref[4:5, :])
    _to_slabs(u_ref, u)
    logits = lax.dot_general(wr_ref[...], u, (((1,), (1,)), ((), ())), precision=HIGHEST,
                             preferred_element_type=F32)
    scores = jax.nn.sigmoid(logits)
    sel = scores + br_ref[...]
    best_val = None
    best_group = None
    for g in range(N_EXPERTS // EXPERTS_PER_GROUP):
        v = [sel[EXPERTS_PER_GROUP * g + i:EXPERTS_PER_GROUP * g + i + 1, :] for i in range(EXPERTS_PER_GROUP)]
        top2 = None
        for i in range(EXPERTS_PER_GROUP):
            for j in range(i + 1, EXPERTS_PER_GROUP):
                pair = v[i] + v[j]
                top2 = pair if top2 is None else jnp.maximum(top2, pair)
        if g == 0:
            best_val, best_group = top2, jnp.zeros_like(top2, dtype=jnp.int32)
        else:
            better = top2 > best_val
            best_group = jnp.where(better, g, best_group)
            best_val = jnp.where(better, top2, best_val)
    eidx = lax.broadcasted_iota(jnp.int32, sel.shape, 0)
    masked = jnp.where(eidx // EXPERTS_PER_GROUP == best_group, sel, -jnp.inf)
    m1 = jnp.max(masked, axis=0, keepdims=True)
    i1 = jnp.min(jnp.where(masked == m1, eidx, N_EXPERTS), axis=0, keepdims=True)
    masked2 = jnp.where(eidx == i1, -jnp.inf, masked)
    m2 = jnp.max(masked2, axis=0, keepdims=True)
    i2 = jnp.min(jnp.where(masked2 == m2, eidx, N_EXPERTS), axis=0, keepdims=True)
    w1 = jnp.sum(jnp.where(eidx == i1, scores, 0.0), axis=0, keepdims=True)
    w2 = jnp.sum(jnp.where(eidx == i2, scores, 0.0), axis=0, keepdims=True)
    denom = w1 + w2
    slot = lax.broadcasted_iota(jnp.int32, idx_ref.shape, 0)
    idx_ref[...] = jnp.where(slot == 0, i1, jnp.where(slot == 1, i2, 0))
    wts_ref[...] = jnp.where(slot == 0, w1 / denom, jnp.where(slot == 1, w2 / denom, 0.0))


def _router(h, mods, mod_index, gain, wr_t, b_router):
    t = h.shape[0]
    tm = ROW_TILE
    return pl.pallas_call(
        _router_kernel,
        grid=(t // tm,),
        in_specs=[pl.BlockSpec((tm, D_MODEL), lambda i: (i, 0)),
                  pl.BlockSpec((None, None, 6, D_MODEL), mod_index),
                  pl.BlockSpec((1, D_MODEL), lambda i: (0, 0)),
                  pl.BlockSpec((N_EXPERTS, D_MODEL), lambda i: (0, 0)),
                  pl.BlockSpec((N_EXPERTS, 1), lambda i: (0, 0))],
        out_specs=[pl.BlockSpec((tm * SLAB, 128), lambda i: (i, 0)),
                   pl.BlockSpec((8, tm), lambda i: (0, i)),
                   pl.BlockSpec((8, tm), lambda i: (0, i))],
        out_shape=[jax.ShapeDtypeStruct((t * SLAB, 128), F32),
                   jax.ShapeDtypeStruct((8, t), jnp.int32),
                   jax.ShapeDtypeStruct((8, t), F32)],
        compiler_params=_params("parallel"),
        name="router",
    )(h, mods, gain, wr_t, b_router)


SLAB = 8
ISSUE_UNROLL = 8


def _to_slabs(ref, value):
    n = value.shape[0]
    for s in range(SLAB):
        ref[pl.ds(s, n, stride=SLAB), :] = value[:, 128 * s:128 * (s + 1)]


def _from_slabs(ref, base, n):
    return jnp.concatenate([ref[pl.ds(base + s, n, stride=SLAB), :] for s in range(SLAB)], axis=1)


def _gather_tile(i, n, idx_hbm, src_hbm, idx_smem, rows, sem_idx, sem_rows):
    tm = ROW_TILE

    def idx_copy(k):
        return pltpu.make_async_copy(idx_hbm.at[k], idx_smem.at[pl.ds((k % 2) * tm, tm)], sem_idx.at[k % 2])

    def token_copy(k, r, src_row):
        dst = pl.multiple_of(((k % 2) * tm + r) * SLAB, SLAB)
        return pltpu.make_async_copy(src_hbm.at[pl.ds(pl.multiple_of(src_row, SLAB), SLAB)],
                                     rows.at[pl.ds(dst, SLAB)], sem_rows.at[k % 2])

    def issue_rows(k):
        def body(blk, carry):
            for j in range(ISSUE_UNROLL):
                r = blk * ISSUE_UNROLL + j
                token_copy(k, r, idx_smem[(k % 2) * tm + r]).start(priority=j % 2)
            return carry
        lax.fori_loop(0, tm // ISSUE_UNROLL, body, 0)

    def wait_rows(k):
        half = pl.ds(pl.multiple_of((k % 2) * tm * SLAB, SLAB), tm * SLAB)
        pltpu.make_async_copy(src_hbm.at[pl.ds(0, tm * SLAB)], rows.at[half], sem_rows.at[k % 2]).wait()

    @pl.when((i == 0) & (n > 0))
    def _():
        idx_copy(0).start()
        idx_copy(0).wait()
        issue_rows(0)

    @pl.when((i == 0) & (n > 1))
    def _():
        idx_copy(1).start()

    @pl.when(i + 1 < n)
    def _():
        idx_copy(i + 1).wait()
        issue_rows(i + 1)

    @pl.when(i + 2 < n)
    def _():
        idx_copy(i + 2).start()

    @pl.when(i < n)
    def _():
        wait_rows(i)


def _experts_kernel(lo_ref, hi_ref, n_ref, idx_hbm, u_hbm, wts_ref, wg_lo, wu_lo, wd_lo, wg_hi, wu_hi, wd_hi,
                    y_ref, idx_smem, rows, sem_idx, sem_rows):
    i = pl.program_id(0)
    n = n_ref[0]
    _gather_tile(i, n, idx_hbm, u_hbm, idx_smem, rows, sem_idx, sem_rows)

    @pl.when(i < n)
    def _():
        u = _from_slabs(rows, (i % 2) * (ROW_TILE * SLAB), ROW_TILE).astype(BF16)

        def expert(wg, wu, wd):
            hid = (_silu(jnp.dot(u, wg[...].astype(BF16), preferred_element_type=F32))
                   * jnp.dot(u, wu[...].astype(BF16), preferred_element_type=F32))
            return jnp.dot(hid.astype(BF16), wd[...].astype(BF16), preferred_element_type=F32)

        _to_slabs(y_ref, wts_ref[:, 0:1] * expert(wg_lo, wu_lo, wd_lo)
                  + wts_ref[:, 1:2] * expert(wg_hi, wu_hi, wd_hi))

    @pl.when(i >= n)
    def _():
        y_ref[...] = jnp.zeros_like(y_ref)


def _experts(tile_lo, tile_hi, n_used, src_idx, u, wts_sorted, w_gate, w_up, w_down):
    n_tiles = src_idx.shape[0]
    tm = ROW_TILE
    w_in_spec = lambda which: pl.BlockSpec((None, D_MODEL, D_EXPERT),
                                           lambda i, lo, hi, n: ((lo, hi)[which][i], 0, 0))
    w_out_spec = lambda which: pl.BlockSpec((None, D_EXPERT, D_MODEL),
                                            lambda i, lo, hi, n: ((lo, hi)[which][i], 0, 0))
    return pl.pallas_call(
        _experts_kernel,
        grid_spec=pltpu.PrefetchScalarGridSpec(
            num_scalar_prefetch=3,
            grid=(n_tiles,),
            in_specs=[pl.BlockSpec(memory_space=pl.ANY),
                      pl.BlockSpec(memory_space=pl.ANY),
                      pl.BlockSpec((tm, 2), lambda i, lo, hi, n: (i, 0)),
                      w_in_spec(0), w_in_spec(0), w_out_spec(0),
                      w_in_spec(1), w_in_spec(1), w_out_spec(1)],
            out_specs=pl.BlockSpec((tm * SLAB, 128), lambda i, lo, hi, n: (i, 0)),
            scratch_shapes=[pltpu.SMEM((2 * tm,), jnp.int32),
                            pltpu.VMEM((2 * tm * SLAB, 128), F32),
                            pltpu.SemaphoreType.DMA((2,)),
                            pltpu.SemaphoreType.DMA((2,))]),
        out_shape=jax.ShapeDtypeStruct((n_tiles * tm * SLAB, 128), F32),
        compiler_params=_params("arbitrary"),
        name="experts",
    )(tile_lo, tile_hi, n_used, src_idx, u, wts_sorted, w_gate, w_up, w_down, w_gate, w_up, w_down)


def _combine_kernel(pos_hbm, y_hbm, h_ref, mod_ref, gain_ref, o_ref, idx_smem, rows, sem_idx, sem_rows,
                    *, final_norm):
    i = pl.program_id(0)
    _gather_tile(i, pl.num_programs(0), pos_hbm, y_hbm, idx_smem, rows, sem_idx, sem_rows)
    out = h_ref[...] + mod_ref[5:6, :] * _from_slabs(rows, (i % 2) * (ROW_TILE * SLAB), ROW_TILE)
    if final_norm:
        ms = jnp.mean(out * out, axis=-1, keepdims=True)
        out = out * lax.rsqrt(ms + EPS) * gain_ref[...]
    o_ref[...] = out


def _combine(pos, y_sorted, h, mods, mod_index, gain, final_norm):
    t = h.shape[0]
    tm = ROW_TILE
    return pl.pallas_call(
        functools.partial(_combine_kernel, final_norm=final_norm),
        grid=(t // tm,),
        in_specs=[pl.BlockSpec(memory_space=pl.ANY),
                  pl.BlockSpec(memory_space=pl.ANY),
                  pl.BlockSpec((tm, D_MODEL), lambda i: (i, 0)),
                  pl.BlockSpec((None, None, 6, D_MODEL), mod_index),
                  pl.BlockSpec((1, D_MODEL), lambda i: (0, 0))],
        out_specs=pl.BlockSpec((tm, D_MODEL), lambda i: (i, 0)),
        out_shape=jax.ShapeDtypeStruct((t, D_MODEL), F32),
        scratch_shapes=[pltpu.SMEM((2 * tm,), jnp.int32),
                        pltpu.VMEM((2 * tm * SLAB, 128), F32),
                        pltpu.SemaphoreType.DMA((2,)),
                        pltpu.SemaphoreType.DMA((2,))],
        compiler_params=_params("arbitrary"),
        name="moe_combine",
    )((pos * SLAB).reshape(t // tm, tm), y_sorted, h, mods, gain)


N_PAIRS = EXPERTS_PER_GROUP * (EXPERTS_PER_GROUP - 1) // 2
N_CLASSES = (N_EXPERTS // EXPERTS_PER_GROUP) * N_PAIRS


def _routing_tables(idx, wts, t):
    tm = ROW_TILE
    n_tiles = t // tm + N_CLASSES
    i1, i2 = idx[0], idx[1]
    lo, hi = jnp.minimum(i1, i2), jnp.maximum(i1, i2)
    first_is_lo = i1 < i2
    w_lo = jnp.where(first_is_lo, wts[0], wts[1])
    w_hi = jnp.where(first_is_lo, wts[1], wts[0])
    pairs = [(a, b) for a in range(EXPERTS_PER_GROUP) for b in range(a + 1, EXPERTS_PER_GROUP)]
    pair_of = [[0] * EXPERTS_PER_GROUP for _ in range(EXPERTS_PER_GROUP)]
    for k, (a, b) in enumerate(pairs):
        pair_of[a][b] = k
    pair_tab = jnp.asarray(pair_of, jnp.int32).reshape(-1)
    cls = (lo // EXPERTS_PER_GROUP) * N_PAIRS + pair_tab[(lo % EXPERTS_PER_GROUP) * EXPERTS_PER_GROUP
                                                         + hi % EXPERTS_PER_GROUP]
    onehot = (cls[:, None] == jnp.arange(N_CLASSES, dtype=jnp.int32)[None, :]).astype(jnp.int32)
    running = jnp.cumsum(onehot, axis=0)
    rank = jnp.sum(running * onehot, axis=1) - 1
    counts = running[-1]
    padded = (counts + tm - 1) // tm * tm
    ends = jnp.cumsum(padded)
    pos = (ends - padded)[cls] + rank
    n_used = (ends[-1] // tm).astype(jnp.int32).reshape(1)
    packed = jnp.stack([jnp.arange(t, dtype=jnp.int32) * SLAB,
                        lax.bitcast_convert_type(w_lo, jnp.int32),
                        lax.bitcast_convert_type(w_hi, jnp.int32)], axis=1)
    packed = jnp.zeros((n_tiles * tm, 3), jnp.int32).at[pos].set(packed)
    src = packed[:, 0]
    wts_sorted = lax.bitcast_convert_type(packed[:, 1:], F32)
    tile_first = jnp.minimum(jnp.arange(n_tiles, dtype=jnp.int32), n_used[0] - 1) * tm
    tile_cls = jnp.minimum(jnp.sum((tile_first[:, None] >= ends[None, :]).astype(jnp.int32), axis=1), N_CLASSES - 1)
    class_lo = jnp.asarray([g * EXPERTS_PER_GROUP + a for g in range(N_EXPERTS // EXPERTS_PER_GROUP)
                            for a, _ in pairs], jnp.int32)
    class_hi = jnp.asarray([g * EXPERTS_PER_GROUP + b for g in range(N_EXPERTS // EXPERTS_PER_GROUP)
                            for _, b in pairs], jnp.int32)
    return class_lo[tile_cls], class_hi[tile_cls], n_used, src.reshape(n_tiles, tm), wts_sorted, pos


def _group_major(t, b, s_len):
    return t[:, :, :2 * HEADS].reshape(b, s_len, 2, GROUPS, HEADS_PER_GROUP)


def kernel(x, c, ctx, c_ctx, w_ada, b_ada, norm_mix, norm_ffn, w_in, ssd_conv_w, ssd_conv_b, ssd_dt_bias,
           ssd_a_log, ssd_d, ssd_norm, w_ssd_out, sc_conv_w, w_sc_out, w_o, w_router, b_router, w_gate, w_up,
           w_down, final_norm):
    b, seq, _ = x.shape
    n_ctx = ctx.shape[1]
    depth = w_ada.shape[0]
    s_len = n_ctx + seq
    t = b * s_len
    nc = s_len // CHUNK
    assert n_ctx == ROW_TILE and seq % ROW_TILE == 0 and ROW_TILE % GRID_W == 0 and ROW_TILE % CHUNK == 0
    tiles_per_batch = s_len // ROW_TILE

    def mod_index(i):
        return (i // tiles_per_batch, jnp.minimum(i % tiles_per_batch, 1), 0, 0)

    rows = -(-(b + 1) // 8) * 8
    cc = jnp.zeros((rows, D_MODEL), F32).at[:b].set(c).at[b].set(c_ctx)
    ada = _ada_params(cc, w_ada, b_ada)
    lat = ada[:, :b].reshape(depth, b, 1, 6, D_MODEL)
    con = jnp.broadcast_to(ada[:, b].reshape(depth, 1, 1, 6, D_MODEL), lat.shape)
    mods_all = jnp.concatenate([con, lat], axis=2)

    spread_mat = _spread_matrix()
    wr_t = w_router.T
    br = b_router.reshape(N_EXPERTS, 1)
    h = jnp.concatenate([ctx, x], axis=1).reshape(t, D_MODEL)

    for l in range(depth):
        mods = mods_all[l]
        w_l = w_in[l]
        w_main = jnp.concatenate([w_l[:, :D_INNER + CONV_DIM], w_l[:, D_INNER + CONV_DIM + 2 * HEADS:]],
                                 axis=1).astype(BF16)
        w_dt = jnp.pad(w_l[:, D_INNER + CONV_DIM:D_INNER + CONV_DIM + 2 * HEADS],
                       ((0, 0), (0, DT_PAD - 2 * HEADS))).astype(BF16)
        dt_bias = jnp.pad(ssd_dt_bias[l].reshape(1, 2 * HEADS), ((0, 0), (0, DT_PAD - 2 * HEADS)))
        alog = jnp.pad(ssd_a_log[l].reshape(1, 2 * HEADS), ((0, 0), (0, DT_PAD - 2 * HEADS)))

        p, dt = _in_proj(h, mods, norm_mix[l].reshape(1, D_MODEL), w_main, w_dt, dt_bias, mod_index)
        xbc = _ssd_conv(p.reshape(b, s_len, P_COLS), ssd_conv_w[l], ssd_conv_b[l].reshape(1, CONV_DIM), n_ctx)
        dt3 = dt.reshape(b, s_len, DT_PAD)
        a2, arow, ea, dw, eal = _decay_tables(dt3, alog)

        cols = jnp.stack([_group_major(v, b, s_len) for v in (a2, ea, dw)], axis=3)
        colslab = cols.transpose(0, 4, 1, 2, 3, 5).reshape(b, GROUPS, s_len, 48)
        rowslab = _group_major(arow, b, s_len).reshape(b, nc, CHUNK, 2, GROUPS, HEADS_PER_GROUP)
        rowslab = rowslab.transpose(0, 4, 3, 1, 5, 2)
        ealx = eal[:, :, 0, :2 * HEADS].reshape(b, nc, 2, GROUPS, HEADS_PER_GROUP).transpose(0, 3, 1, 2, 4)
        ealx = jnp.repeat(ealx, HEAD_DIM, axis=-1)

        y = _ssd_scan(xbc, colslab, rowslab, ealx, spread_mat, n_ctx)
        last = l == depth - 1
        h = _mixer_out(h, y.reshape(t, D_INNER), xbc.reshape(t, CONV_DIM), p, mods, mod_index,
                       jnp.repeat(ssd_d[l], HEAD_DIM).reshape(1, D_INNER), ssd_norm[l].reshape(1, D_INNER),
                       w_ssd_out[l].astype(BF16), sc_conv_w[l], w_sc_out[l].astype(BF16), w_o[l].astype(BF16),
                       tiles_per_batch, n_ctx, latent_only=last)
        ffn_index = (lambda i: (i // (seq // ROW_TILE), 1, 0, 0)) if last else mod_index
        t_ffn = h.shape[0]
        u2, idx, wts = _router(h, mods, ffn_index, norm_ffn[l].reshape(1, D_MODEL), wr_t, br)
        tile_lo, tile_hi, n_used, src_idx, wts_sorted, pos = _routing_tables(idx, wts, t_ffn)
        y_sorted = _experts(tile_lo, tile_hi, n_used, src_idx, u2, wts_sorted,
                            w_gate[l], w_up[l], w_down[l])
        h = _combine(pos, y_sorted, h, mods, ffn_index, final_norm.reshape(1, D_MODEL), final_norm=last)

    return h.reshape(b, seq, D_MODEL)
```

```python
import functools

import jax
import jax.numpy as jnp
import numpy as np
from jax import lax
from jax.experimental import pallas as pl
from jax.experimental.pallas import tpu as pltpu

F32 = jnp.float32
BF16 = jnp.bfloat16
HIGHEST = lax.Precision.HIGHEST

D_MODEL = 1024
D_INNER = 2048
HEAD_DIM = 64
HEADS = 32
GROUPS = 4
HEADS_PER_GROUP = HEADS // GROUPS
GROUP_WIDTH = HEADS_PER_GROUP * HEAD_DIM
STATE = 128
CHUNK = 128
CONV_DIM = D_INNER + 2 * GROUPS * STATE
SSD_CONV = 4
SC_CONV = 3
GRID_W = 64
N_EXPERTS = 16
EXPERTS_PER_GROUP = 4
D_EXPERT = 512
EPS = 1e-6
P_COLS = 10 * D_MODEL
DT_PAD = 128
ROW_TILE = 256
VMEM_LIMIT = 56 * 1024 * 1024


def _params(*sem):
    return pltpu.CompilerParams(dimension_semantics=sem, vmem_limit_bytes=VMEM_LIMIT)


def _silu(v):
    return v * jax.nn.sigmoid(v)


def _modulated_norm(x, gain, shift, scale):
    ms = jnp.mean(x * x, axis=-1, keepdims=True)
    return x * lax.rsqrt(ms + EPS) * gain * (1.0 + scale) + shift


def _ada_kernel(c_ref, w_ref, b_ref, o_ref):
    cond = _silu(c_ref[...])
    o_ref[...] = jnp.dot(cond, w_ref[...], precision=HIGHEST, preferred_element_type=F32) + b_ref[...]


def _ada_params(cc, w_ada, b_ada):
    depth, _, n = w_ada.shape
    rows = cc.shape[0]
    tn = 512
    return pl.pallas_call(
        _ada_kernel,
        grid=(depth, n // tn),
        in_specs=[pl.BlockSpec((rows, D_MODEL), lambda l, j: (0, 0)),
                  pl.BlockSpec((None, D_MODEL, tn), lambda l, j: (l, 0, j)),
                  pl.BlockSpec((None, 1, tn), lambda l, j: (l, 0, j))],
        out_specs=pl.BlockSpec((None, rows, tn), lambda l, j: (l, 0, j)),
        out_shape=jax.ShapeDtypeStruct((depth, rows, n), F32),
        compiler_params=_params("parallel", "parallel"),
        name="ada_params",
    )(cc, w_ada, b_ada.reshape(depth, 1, n))


def _in_proj_kernel(h_ref, mod_ref, gain_ref, w_ref, wdt_ref, dtb_ref, p_ref, dt_ref):
    u = _modulated_norm(h_ref[...], gain_ref[...], mod_ref[0:1, :], mod_ref[1:2, :]).astype(BF16)
    for n0 in range(0, P_COLS, 512):
        p_ref[:, n0:n0 + 512] = jnp.dot(u, w_ref[:, n0:n0 + 512], preferred_element_type=F32).astype(BF16)
    raw = jnp.dot(u, wdt_ref[...], preferred_element_type=F32) + dtb_ref[...]
    dt_ref[...] = jnp.maximum(raw, 0.0) + jnp.log(1.0 + jnp.exp(-jnp.abs(raw)))


def _in_proj(h, mods, gain, w_main, w_dt, dt_bias, mod_index):
    t = h.shape[0]
    return pl.pallas_call(
        _in_proj_kernel,
        grid=(t // ROW_TILE,),
        in_specs=[pl.BlockSpec((ROW_TILE, D_MODEL), lambda i: (i, 0)),
                  pl.BlockSpec((None, None, 6, D_MODEL), mod_index),
                  pl.BlockSpec((1, D_MODEL), lambda i: (0, 0)),
                  pl.BlockSpec((D_MODEL, P_COLS), lambda i: (0, 0), pipeline_mode=pl.Buffered(1)),
                  pl.BlockSpec((D_MODEL, DT_PAD), lambda i: (0, 0)),
                  pl.BlockSpec((1, DT_PAD), lambda i: (0, 0))],
        out_specs=[pl.BlockSpec((ROW_TILE, P_COLS), lambda i: (i, 0)),
                   pl.BlockSpec((ROW_TILE, DT_PAD), lambda i: (i, 0))],
        out_shape=[jax.ShapeDtypeStruct((t, P_COLS), BF16),
                   jax.ShapeDtypeStruct((t, DT_PAD), F32)],
        compiler_params=_params("parallel"),
        name="in_proj",
    )(h, mods, gain, w_main, w_dt, dt_bias)


CONV_TAPS = (-1, 1, 2)
CONV_WINDOW = 2 * CHUNK


def _conv_shift_plan(s_len, n_ctx):
    mats, plan, seen = [], [], {}
    for c in range(s_len // CHUNK):
        r0 = c * CHUNK
        start = min(max(r0 - CHUNK // 2, 0), s_len - CONV_WINDOW)
        sel = np.zeros((len(CONV_TAPS) * CHUNK, CONV_WINDOW), np.float32)
        for ti, off in enumerate(CONV_TAPS):
            for l in range(CHUNK):
                dst, src = r0 + l, r0 + l + off
                if 0 <= src < s_len and (src >= n_ctx) == (dst >= n_ctx):
                    sel[ti * CHUNK + l, src - start] = 1.0
        key = sel.tobytes()
        if key not in seen:
            seen[key] = len(mats)
            mats.append(sel)
        plan.append((start, seen[key]))
    return jnp.asarray(np.stack(mats), BF16), tuple(plan)


def _ssd_conv_kernel(x_ref, sel_ref, w_ref, b_ref, o_ref, *, plan):
    for c, (start, kind) in enumerate(plan):
        rows = slice(c * CHUNK, (c + 1) * CHUNK)
        window = x_ref[start:start + CONV_WINDOW, :]
        acc = x_ref[rows, :].astype(F32) * w_ref[1:2, :] + b_ref[...]
        for ti, k in enumerate((0, 2, 3)):
            sh = jnp.dot(sel_ref[kind, ti * CHUNK:(ti + 1) * CHUNK, :], window, preferred_element_type=F32)
            acc = acc + sh * w_ref[k:k + 1, :]
        o_ref[rows, :] = _silu(acc.astype(BF16))


def _ssd_conv(p3, conv_w, conv_b, n_ctx):
    b, s_len, _ = p3.shape
    tc = 256
    first = D_INNER // tc
    sel, plan = _conv_shift_plan(s_len, n_ctx)
    return pl.pallas_call(
        functools.partial(_ssd_conv_kernel, plan=plan),
        grid=(b, CONV_DIM // tc),
        in_specs=[pl.BlockSpec((None, s_len, tc), lambda i, c: (i, 0, first + c)),
                  pl.BlockSpec(sel.shape, lambda i, c: (0, 0, 0)),
                  pl.BlockSpec((SSD_CONV, tc), lambda i, c: (0, c)),
                  pl.BlockSpec((1, tc), lambda i, c: (0, c))],
        out_specs=pl.BlockSpec((None, s_len, tc), lambda i, c: (i, 0, c)),
        out_shape=jax.ShapeDtypeStruct((b, s_len, CONV_DIM), BF16),
        compiler_params=_params("parallel", "parallel"),
        name="ssd_conv",
    )(p3, sel, conv_w, conv_b)


LOG2E = 1.4426950408889634


def _decay_kernel(dt_ref, alog_ref, a2_ref, arow_ref, ea_ref, dw_ref, eal_ref, *, n_chunks):
    neg_a = -jnp.exp(alog_ref[...])
    r = lax.broadcasted_iota(jnp.int32, (CHUNK, CHUNK), 0)
    c = lax.broadcasted_iota(jnp.int32, (CHUNK, CHUNK), 1)
    lower = (c <= r).astype(F32)
    upper = (c >= r).astype(F32)
    is_fwd = lax.broadcasted_iota(jnp.int32, (1, DT_PAD), 1) < HEADS

    def body(ci, carry):
        rows = pl.ds(pl.multiple_of(ci * CHUNK, CHUNK), CHUNK)
        dt = dt_ref[rows, :]
        da = dt * neg_a
        fwd = jnp.dot(lower, da, precision=HIGHEST, preferred_element_type=F32)
        bwd = jnp.dot(upper, da, precision=HIGHEST, preferred_element_type=F32)
        a = jnp.where(is_fwd, fwd, bwd)
        total = jnp.where(is_fwd, fwd[CHUNK - 1:CHUNK, :], bwd[0:1, :])
        a2 = a * LOG2E
        a2_ref[rows, :] = a2
        arow_ref[rows, :] = a2 - jnp.log(dt) * LOG2E
        ea_ref[rows, :] = jnp.exp(a)
        dw_ref[rows, :] = dt * jnp.exp(total - a)
        eal_ref[ci] = jnp.exp(total)
        return carry

    lax.fori_loop(0, n_chunks, body, 0)


def _decay_tables(dt3, alog):
    b, s_len, _ = dt3.shape
    nc = s_len // CHUNK
    tile = pl.BlockSpec((None, s_len, DT_PAD), lambda i: (i, 0, 0))
    full = jax.ShapeDtypeStruct((b, s_len, DT_PAD), F32)
    return pl.pallas_call(
        functools.partial(_decay_kernel, n_chunks=nc),
        grid=(b,),
        in_specs=[tile, pl.BlockSpec((1, DT_PAD), lambda i: (0, 0))],
        out_specs=[tile, tile, tile, tile, pl.BlockSpec((None, nc, 1, DT_PAD), lambda i: (i, 0, 0, 0))],
        out_shape=[full, full, full, full, jax.ShapeDtypeStruct((b, nc, 1, DT_PAD), F32)],
        compiler_params=_params("parallel"),
        name="ssd_decay",
    )(dt3, alog)


def _scan_kernel(x_ref, b_ref, c_ref, col_ref, row_ref, eal_ref, exp_ref, y_ref,
                 state_f, state_b, yacc, *, n_chunks, n_ctx_chunks):
    yacc[...] = jnp.zeros_like(yacc)
    state_f[...] = jnp.zeros_like(state_f)
    state_b[...] = jnp.zeros_like(state_b)
    r = lax.broadcasted_iota(jnp.int32, (CHUNK, CHUNK), 0)
    c = lax.broadcasted_iota(jnp.int32, (CHUNK, CHUNK), 1)
    mask_bias = tuple(jnp.where(keep, 0.0, -jnp.inf).astype(BF16) for keep in (r >= c, r <= c))
    low_half = lax.broadcasted_iota(jnp.int32, (1, 2 * HEAD_DIM), 1) < HEAD_DIM
    states = (state_f, state_b)

    def chunk_step(ci, d):
        r0 = pl.multiple_of(ci * CHUNK, CHUNK)
        xc = x_ref[pl.ds(r0, CHUNK), :]
        bc = b_ref[pl.ds(r0, CHUNK), :]
        cc = c_ref[pl.ds(r0, CHUNK), :]
        cols = col_ref[pl.ds(r0, CHUNK), :]
        arow = row_ref[d, ci]
        cb = lax.dot_general(cc, bc, (((1,), (1,)), ((), ())), preferred_element_type=F32).astype(BF16)
        spread = jnp.dot(cols.astype(BF16), exp_ref[d], preferred_element_type=F32)
        ea_x = spread[:, :GROUP_WIDTH]
        dw_x = spread[:, GROUP_WIDTH:]
        st = states[d][...]
        y = jnp.dot(cc, st.astype(BF16), preferred_element_type=F32) * ea_x
        parts = []
        for k in range(HEADS_PER_GROUP // 2):
            lhs = []
            for j in (2 * k, 2 * k + 1):
                col = 24 * d + j
                seg = (cols[:, col:col + 1] - arow[j:j + 1, :]).astype(BF16)
                lhs.append(jnp.exp2(seg + mask_bias[d]) * cb)
            xp = xc[:, 128 * k:128 * (k + 1)]
            zero = jnp.zeros_like(xp)
            rhs = jnp.concatenate([jnp.where(low_half, xp, zero), jnp.where(low_half, zero, xp)], axis=0)
            parts.append(jnp.dot(jnp.concatenate(lhs, axis=1), rhs, preferred_element_type=F32))
        y = y + jnp.concatenate(parts, axis=1)
        yacc[pl.ds(r0, CHUNK), :] += y
        xw = (xc.astype(F32) * dw_x).astype(BF16)
        upd = lax.dot_general(bc, xw, (((0,), (0,)), ((), ())), preferred_element_type=F32)
        states[d][...] = st * eal_ref[ci, pl.ds(d, 1), :] + upd

    def body(i, carry):
        chunk_step(i, 0)
        cb_idx = jnp.where(i < n_ctx_chunks, n_ctx_chunks - 1 - i, n_chunks + n_ctx_chunks - 1 - i)
        chunk_step(cb_idx, 1)
        return carry

    lax.fori_loop(0, n_chunks, body, 0, unroll=2)
    y_ref[...] = yacc[...].astype(BF16)


def _ssd_scan(xbc, colslab, rowslab, ealx, spread_mat, n_ctx):
    b, s_len, _ = xbc.shape
    nc = s_len // CHUNK
    x_blocks = D_INNER // STATE
    return pl.pallas_call(
        functools.partial(_scan_kernel, n_chunks=nc, n_ctx_chunks=n_ctx // CHUNK),
        grid=(b, GROUPS),
        in_specs=[pl.BlockSpec((None, s_len, GROUP_WIDTH), lambda i, g: (i, 0, g)),
                  pl.BlockSpec((None, s_len, STATE), lambda i, g: (i, 0, x_blocks + g)),
                  pl.BlockSpec((None, s_len, STATE), lambda i, g: (i, 0, x_blocks + GROUPS + g)),
                  pl.BlockSpec((None, None, s_len, 48), lambda i, g: (i, g, 0, 0)),
                  pl.BlockSpec((None, None, 2, nc, HEADS_PER_GROUP, CHUNK), lambda i, g: (i, g, 0, 0, 0, 0)),
                  pl.BlockSpec((None, None, nc, 2, GROUP_WIDTH), lambda i, g: (i, g, 0, 0, 0)),
                  pl.BlockSpec((2, 48, 2 * GROUP_WIDTH), lambda i, g: (0, 0, 0))],
        out_specs=pl.BlockSpec((None, s_len, GROUP_WIDTH), lambda i, g: (i, 0, g)),
        out_shape=jax.ShapeDtypeStruct((b, s_len, D_INNER), BF16),
        scratch_shapes=[pltpu.VMEM((STATE, GROUP_WIDTH), F32),
                        pltpu.VMEM((STATE, GROUP_WIDTH), F32),
                        pltpu.VMEM((s_len, GROUP_WIDTH), F32)],
        compiler_params=_params("parallel", "parallel"),
        name="ssd_scan",
    )(xbc, xbc, xbc, colslab, rowslab, ealx, spread_mat)


def _spread_matrix():
    rows = jnp.arange(48)[:, None]
    lanes = jnp.arange(2 * GROUP_WIDTH)[None, :]
    mats = []
    for d in range(2):
        kind = (rows - 24 * d) // HEADS_PER_GROUP
        head = (rows - 24 * d) % HEADS_PER_GROUP
        valid = (rows >= 24 * d + 8) & (rows < 24 * d + 24)
        hit = valid & (lanes // GROUP_WIDTH == kind - 1) & ((lanes % GROUP_WIDTH) // HEAD_DIM == head)
        mats.append(hit.astype(BF16))
    return jnp.stack(mats)


def _mixer_out_kernel(h_ref, y_ref, x_ref, z_ref, scb_ref, scc_ref, sch_ref, ga_ref, gb_ref, mod_ref,
                      dskip_ref, gain_ref, wssd_ref, scw_ref, wsc_ref, wo_ref, o_ref,
                      *, tile_of, tiles_per_batch, n_ctx):
    v = scc_ref[...].astype(F32) * sch_ref[...].astype(F32)
    tm = v.shape[0]
    pos = (tile_of(pl.program_id(0)) % tiles_per_batch) * tm + lax.broadcasted_iota(jnp.int32, (tm, 1), 0)
    in_ctx = pos < n_ctx
    col = (pos - n_ctx) % GRID_W
    first = jnp.where(in_ctx, pos, col)
    last = jnp.where(in_ctx, n_ctx - 1 - pos, GRID_W - 1 - col)
    left_ok = first != 0
    right_ok = last != 0
    vl = jnp.where(left_ok, pltpu.roll(v, 1, 0), 0.0)
    vr = jnp.where(right_ok, pltpu.roll(v, tm - 1, 0), 0.0)
    cv = vl * scw_ref[0:1, :] + v * scw_ref[1:2, :] + vr * scw_ref[2:3, :]
    gated = (scb_ref[...].astype(F32) * cv).astype(BF16)

    half = tm // 2
    for r0 in (0, half):
        rows = slice(r0, r0 + half)
        yv = y_ref[rows, :].astype(F32) + x_ref[rows, :].astype(F32) * dskip_ref[...]
        yz = yv * _silu(z_ref[rows, :]).astype(F32)
        ms = jnp.mean(yz * yz, axis=-1, keepdims=True)
        yn = (yz * lax.rsqrt(ms + EPS) * gain_ref[...]).astype(BF16)
        ssd = jnp.dot(yn, wssd_ref[...], preferred_element_type=F32)
        sc = jnp.dot(gated[rows, :], wsc_ref[...], preferred_element_type=F32)
        mix = (jax.nn.sigmoid(ga_ref[rows, :]) * ssd.astype(BF16)
               + jax.nn.sigmoid(gb_ref[rows, :]) * sc.astype(BF16))
        out = jnp.dot(mix, wo_ref[...], preferred_element_type=F32)
        o_ref[rows, :] = h_ref[rows, :] + mod_ref[2:3, :] * out


def _mixer_out(h, y, xbc, p, mods, mod_index, d_skip, gain, w_ssd, sc_w, w_sc, w_o, tiles_per_batch, n_ctx,
               latent_only):
    t = h.shape[0]
    tm = ROW_TILE
    skip = n_ctx // tm if latent_only else 0
    per_batch = tiles_per_batch - skip

    def tile_of(i):
        return (i // per_batch) * tiles_per_batch + skip + i % per_batch

    row = lambda width, j: pl.BlockSpec((tm, width), lambda i: (tile_of(i), j))
    const = lambda shape: pl.BlockSpec(shape, lambda i: (0,) * len(shape))
    n_steps = (t // tm) // tiles_per_batch * per_batch
    return pl.pallas_call(
        functools.partial(_mixer_out_kernel, tile_of=tile_of, tiles_per_batch=tiles_per_batch, n_ctx=n_ctx),
        grid=(n_steps,),
        in_specs=[row(D_MODEL, 0), row(D_INNER, 0), row(D_INNER, 0), row(D_INNER, 0),
                  row(D_MODEL, 5), row(D_MODEL, 6), row(D_MODEL, 7), row(D_MODEL, 8), row(D_MODEL, 9),
                  pl.BlockSpec((None, None, 6, D_MODEL), lambda i: mod_index(tile_of(i))),
                  const((1, D_INNER)), const((1, D_INNER)), const((D_INNER, D_MODEL)),
                  const((SC_CONV, D_MODEL)), const((D_MODEL, D_MODEL)), const((D_MODEL, D_MODEL))],
        out_specs=pl.BlockSpec((tm, D_MODEL), lambda i: (i, 0)),
        out_shape=jax.ShapeDtypeStruct((n_steps * tm, D_MODEL), F32),
        compiler_params=_params("parallel"),
        name="mixer_out",
    )(h, y, xbc, p, p, p, p, p, p, mods, d_skip, gain, w_ssd, sc_w, w_sc, w_o)


def _router_kernel(h_ref, mod_ref, gain_ref, wr_ref, br_ref, u_ref, idx_ref, wts_ref):
    u = _modulated_norm(h_ref[...], gain_ref[...], mod_ref[3:4, :], mod_ref[4:5, :])
    _to_slabs(u_ref, u)
    logits = lax.dot_general(wr_ref[...], u, (((1,), (1,)), ((), ())), precision=HIGHEST,
                             preferred_element_type=F32)
    scores = jax.nn.sigmoid(logits)
    sel = scores + br_ref[...]
    best_val = None
    best_group = None
    for g in range(N_EXPERTS // EXPERTS_PER_GROUP):
        v = [sel[EXPERTS_PER_GROUP * g + i:EXPERTS_PER_GROUP * g + i + 1, :] for i in range(EXPERTS_PER_GROUP)]
        top2 = None
        for i in range(EXPERTS_PER_GROUP):
            for j in range(i + 1, EXPERTS_PER_GROUP):
                pair = v[i] + v[j]
                top2 = pair if top2 is None else jnp.maximum(top2, pair)
        if g == 0:
            best_val, best_group = top2, jnp.zeros_like(top2, dtype=jnp.int32)
        else:
            better = top2 > best_val
            best_group = jnp.where(better, g, best_group)
            best_val = jnp.where(better, top2, best_val)
    eidx = lax.broadcasted_iota(jnp.int32, sel.shape, 0)
    masked = jnp.where(eidx // EXPERTS_PER_GROUP == best_group, sel, -jnp.inf)
    m1 = jnp.max(masked, axis=0, keepdims=True)
    i1 = jnp.min(jnp.where(masked == m1, eidx, N_EXPERTS), axis=0, keepdims=True)
    masked2 = jnp.where(eidx == i1, -jnp.inf, masked)
    m2 = jnp.max(masked2, axis=0, keepdims=True)
    i2 = jnp.min(jnp.where(masked2 == m2, eidx, N_EXPERTS), axis=0, keepdims=True)
    w1 = jnp.sum(jnp.where(eidx == i1, scores, 0.0), axis=0, keepdims=True)
    w2 = jnp.sum(jnp.where(eidx == i2, scores, 0.0), axis=0, keepdims=True)
    denom = w1 + w2
    slot = lax.broadcasted_iota(jnp.int32, idx_ref.shape, 0)
    idx_ref[...] = jnp.where(slot == 0, i1, jnp.where(slot == 1, i2, 0))
    wts_ref[...] = jnp.where(slot == 0, w1 / denom, jnp.where(slot == 1, w2 / denom, 0.0))


def _router(h, mods, mod_index, gain, wr_t, b_router):
    t = h.shape[0]
    tm = ROW_TILE
    return pl.pallas_call(
        _router_kernel,
        grid=(t // tm,),
        in_specs=[pl.BlockSpec((tm, D_MODEL), lambda i: (i, 0)),
                  pl.BlockSpec((None, None, 6, D_MODEL), mod_index),
                  pl.BlockSpec((1, D_MODEL), lambda i: (0, 0)),
                  pl.BlockSpec((N_EXPERTS, D_MODEL), lambda i: (0, 0)),
                  pl.BlockSpec((N_EXPERTS, 1), lambda i: (0, 0))],
        out_specs=[pl.BlockSpec((tm * SLAB, 128), lambda i: (i, 0)),
                   pl.BlockSpec((8, tm), lambda i: (0, i)),
                   pl.BlockSpec((8, tm), lambda i: (0, i))],
        out_shape=[jax.ShapeDtypeStruct((t * SLAB, 128), F32),
                   jax.ShapeDtypeStruct((8, t), jnp.int32),
                   jax.ShapeDtypeStruct((8, t), F32)],
        compiler_params=_params("parallel"),
        name="router",
    )(h, mods, gain, wr_t, b_router)


SLAB = 8
ISSUE_UNROLL = 8


def _to_slabs(ref, value):
    n = value.shape[0]
    for s in range(SLAB):
        ref[pl.ds(s, n, stride=SLAB), :] = value[:, 128 * s:128 * (s + 1)]


def _from_slabs(ref, base, n):
    return jnp.concatenate([ref[pl.ds(base + s, n, stride=SLAB), :] for s in range(SLAB)], axis=1)


def _gather_tile(i, n, idx_hbm, src_hbm, idx_smem, rows, sem_idx, sem_rows):
    tm = ROW_TILE

    def idx_copy(k):
        return pltpu.make_async_copy(idx_hbm.at[k], idx_smem.at[pl.ds((k % 2) * tm, tm)], sem_idx.at[k % 2])

    def token_copy(k, r, src_row):
        dst = pl.multiple_of(((k % 2) * tm + r) * SLAB, SLAB)
        return pltpu.make_async_copy(src_hbm.at[pl.ds(pl.multiple_of(src_row, SLAB), SLAB)],
                                     rows.at[pl.ds(dst, SLAB)], sem_rows.at[k % 2])

    def issue_rows(k):
        def body(blk, carry):
            for j in range(ISSUE_UNROLL):
                r = blk * ISSUE_UNROLL + j
                token_copy(k, r, idx_smem[(k % 2) * tm + r]).start(priority=j % 2)
            return carry
        lax.fori_loop(0, tm // ISSUE_UNROLL, body, 0)

    def wait_rows(k):
        half = pl.ds(pl.multiple_of((k % 2) * tm * SLAB, SLAB), tm * SLAB)
        pltpu.make_async_copy(src_hbm.at[pl.ds(0, tm * SLAB)], rows.at[half], sem_rows.at[k % 2]).wait()

    @pl.when((i == 0) & (n > 0))
    def _():
        idx_copy(0).start()
        idx_copy(0).wait()
        issue_rows(0)

    @pl.when((i == 0) & (n > 1))
    def _():
        idx_copy(1).start()

    @pl.when(i + 1 < n)
    def _():
        idx_copy(i + 1).wait()
        issue_rows(i + 1)

    @pl.when(i + 2 < n)
    def _():
        idx_copy(i + 2).start()

    @pl.when(i < n)
    def _():
        wait_rows(i)


def _experts_kernel(lo_ref, hi_ref, n_ref, idx_hbm, u_hbm, wts_ref, wg_lo, wu_lo, wd_lo, wg_hi, wu_hi, wd_hi,
                    y_ref, idx_smem, rows, sem_idx, sem_rows):
    i = pl.program_id(0)
    n = n_ref[0]
    _gather_tile(i, n, idx_hbm, u_hbm, idx_smem, rows, sem_idx, sem_rows)

    @pl.when(i < n)
    def _():
        u = _from_slabs(rows, (i % 2) * (ROW_TILE * SLAB), ROW_TILE).astype(BF16)

        def expert(wg, wu, wd):
            hid = _silu(jnp.dot(u, wg[...], preferred_element_type=F32)) * jnp.dot(u, wu[...], preferred_element_type=F32)
            return jnp.dot(hid.astype(BF16), wd[...], preferred_element_type=F32)

        _to_slabs(y_ref, wts_ref[:, 0:1] * expert(wg_lo, wu_lo, wd_lo)
                  + wts_ref[:, 1:2] * expert(wg_hi, wu_hi, wd_hi))

    @pl.when(i >= n)
    def _():
        y_ref[...] = jnp.zeros_like(y_ref)


def _experts(tile_lo, tile_hi, n_used, src_idx, u, wts_sorted, w_gate, w_up, w_down):
    n_tiles = src_idx.shape[0]
    tm = ROW_TILE
    w_in_spec = lambda which: pl.BlockSpec((None, D_MODEL, D_EXPERT),
                                           lambda i, lo, hi, n: ((lo, hi)[which][i], 0, 0))
    w_out_spec = lambda which: pl.BlockSpec((None, D_EXPERT, D_MODEL),
                                            lambda i, lo, hi, n: ((lo, hi)[which][i], 0, 0))
    return pl.pallas_call(
        _experts_kernel,
        grid_spec=pltpu.PrefetchScalarGridSpec(
            num_scalar_prefetch=3,
            grid=(n_tiles,),
            in_specs=[pl.BlockSpec(memory_space=pl.ANY),
                      pl.BlockSpec(memory_space=pl.ANY),
                      pl.BlockSpec((tm, 2), lambda i, lo, hi, n: (i, 0)),
                      w_in_spec(0), w_in_spec(0), w_out_spec(0),
                      w_in_spec(1), w_in_spec(1), w_out_spec(1)],
            out_specs=pl.BlockSpec((tm * SLAB, 128), lambda i, lo, hi, n: (i, 0)),
            scratch_shapes=[pltpu.SMEM((2 * tm,), jnp.int32),
                            pltpu.VMEM((2 * tm * SLAB, 128), F32),
                            pltpu.SemaphoreType.DMA((2,)),
                            pltpu.SemaphoreType.DMA((2,))]),
        out_shape=jax.ShapeDtypeStruct((n_tiles * tm * SLAB, 128), F32),
        compiler_params=_params("arbitrary"),
        name="experts",
    )(tile_lo, tile_hi, n_used, src_idx, u, wts_sorted, w_gate, w_up, w_down, w_gate, w_up, w_down)


def _combine_kernel(pos_hbm, y_hbm, h_ref, mod_ref, gain_ref, o_ref, idx_smem, rows, sem_idx, sem_rows,
                    *, final_norm):
    i = pl.program_id(0)
    _gather_tile(i, pl.num_programs(0), pos_hbm, y_hbm, idx_smem, rows, sem_idx, sem_rows)
    out = h_ref[...] + mod_ref[5:6, :] * _from_slabs(rows, (i % 2) * (ROW_TILE * SLAB), ROW_TILE)
    if final_norm:
        ms = jnp.mean(out * out, axis=-1, keepdims=True)
        out = out * lax.rsqrt(ms + EPS) * gain_ref[...]
    o_ref[...] = out


def _combine(pos, y_sorted, h, mods, mod_index, gain, final_norm):
    t = h.shape[0]
    tm = ROW_TILE
    return pl.pallas_call(
        functools.partial(_combine_kernel, final_norm=final_norm),
        grid=(t // tm,),
        in_specs=[pl.BlockSpec(memory_space=pl.ANY),
                  pl.BlockSpec(memory_space=pl.ANY),
                  pl.BlockSpec((tm, D_MODEL), lambda i: (i, 0)),
                  pl.BlockSpec((None, None, 6, D_MODEL), mod_index),
                  pl.BlockSpec((1, D_MODEL), lambda i: (0, 0))],
        out_specs=pl.BlockSpec((tm, D_MODEL), lambda i: (i, 0)),
        out_shape=jax.ShapeDtypeStruct((t, D_MODEL), F32),
        scratch_shapes=[pltpu.SMEM((2 * tm,), jnp.int32),
                        pltpu.VMEM((2 * tm * SLAB, 128), F32),
                        pltpu.SemaphoreType.DMA((2,)),
                        pltpu.SemaphoreType.DMA((2,))],
        compiler_params=_params("arbitrary"),
        name="moe_combine",
    )((pos * SLAB).reshape(t // tm, tm), y_sorted, h, mods, gain)


N_PAIRS = EXPERTS_PER_GROUP * (EXPERTS_PER_GROUP - 1) // 2
N_CLASSES = (N_EXPERTS // EXPERTS_PER_GROUP) * N_PAIRS


def _routing_tables(idx, wts, t):
    tm = ROW_TILE
    n_tiles = t // tm + N_CLASSES
    i1, i2 = idx[0], idx[1]
    lo, hi = jnp.minimum(i1, i2), jnp.maximum(i1, i2)
    first_is_lo = i1 < i2
    w_lo = jnp.where(first_is_lo, wts[0], wts[1])
    w_hi = jnp.where(first_is_lo, wts[1], wts[0])
    pairs = [(a, b) for a in range(EXPERTS_PER_GROUP) for b in range(a + 1, EXPERTS_PER_GROUP)]
    pair_of = [[0] * EXPERTS_PER_GROUP for _ in range(EXPERTS_PER_GROUP)]
    for k, (a, b) in enumerate(pairs):
        pair_of[a][b] = k
    pair_tab = jnp.asarray(pair_of, jnp.int32).reshape(-1)
    cls = (lo // EXPERTS_PER_GROUP) * N_PAIRS + pair_tab[(lo % EXPERTS_PER_GROUP) * EXPERTS_PER_GROUP
                                                         + hi % EXPERTS_PER_GROUP]
    onehot = (cls[:, None] == jnp.arange(N_CLASSES, dtype=jnp.int32)[None, :]).astype(jnp.int32)
    running = jnp.cumsum(onehot, axis=0)
    rank = jnp.sum(running * onehot, axis=1) - 1
    counts = running[-1]
    padded = (counts + tm - 1) // tm * tm
    ends = jnp.cumsum(padded)
    pos = (ends - padded)[cls] + rank
    n_used = (ends[-1] // tm).astype(jnp.int32).reshape(1)
    packed = jnp.stack([jnp.arange(t, dtype=jnp.int32) * SLAB,
                        lax.bitcast_convert_type(w_lo, jnp.int32),
                        lax.bitcast_convert_type(w_hi, jnp.int32)], axis=1)
    packed = jnp.zeros((n_tiles * tm, 3), jnp.int32).at[pos].set(packed)
    src = packed[:, 0]
    wts_sorted = lax.bitcast_convert_type(packed[:, 1:], F32)
    tile_first = jnp.minimum(jnp.arange(n_tiles, dtype=jnp.int32), n_used[0] - 1) * tm
    tile_cls = jnp.minimum(jnp.sum((tile_first[:, None] >= ends[None, :]).astype(jnp.int32), axis=1), N_CLASSES - 1)
    class_lo = jnp.asarray([g * EXPERTS_PER_GROUP + a for g in range(N_EXPERTS // EXPERTS_PER_GROUP)
                            for a, _ in pairs], jnp.int32)
    class_hi = jnp.asarray([g * EXPERTS_PER_GROUP + b for g in range(N_EXPERTS // EXPERTS_PER_GROUP)
                            for _, b in pairs], jnp.int32)
    return class_lo[tile_cls], class_hi[tile_cls], n_used, src.reshape(n_tiles, tm), wts_sorted, pos


def _group_major(t, b, s_len):
    return t[:, :, :2 * HEADS].reshape(b, s_len, 2, GROUPS, HEADS_PER_GROUP)


def kernel(x, c, ctx, c_ctx, w_ada, b_ada, norm_mix, norm_ffn, w_in, ssd_conv_w, ssd_conv_b, ssd_dt_bias,
           ssd_a_log, ssd_d, ssd_norm, w_ssd_out, sc_conv_w, w_sc_out, w_o, w_router, b_router, w_gate, w_up,
           w_down, final_norm):
    b, seq, _ = x.shape
    n_ctx = ctx.shape[1]
    depth = w_ada.shape[0]
    s_len = n_ctx + seq
    t = b * s_len
    nc = s_len // CHUNK
    assert n_ctx == ROW_TILE and seq % ROW_TILE == 0 and ROW_TILE % GRID_W == 0 and ROW_TILE % CHUNK == 0
    tiles_per_batch = s_len // ROW_TILE

    def mod_index(i):
        return (i // tiles_per_batch, jnp.minimum(i % tiles_per_batch, 1), 0, 0)

    rows = -(-(b + 1) // 8) * 8
    cc = jnp.zeros((rows, D_MODEL), F32).at[:b].set(c).at[b].set(c_ctx)
    ada = _ada_params(cc, w_ada, b_ada)
    lat = ada[:, :b].reshape(depth, b, 1, 6, D_MODEL)
    con = jnp.broadcast_to(ada[:, b].reshape(depth, 1, 1, 6, D_MODEL), lat.shape)
    mods_all = jnp.concatenate([con, lat], axis=2)

    spread_mat = _spread_matrix()
    wr_t = w_router.T
    br = b_router.reshape(N_EXPERTS, 1)
    h = jnp.concatenate([ctx, x], axis=1).reshape(t, D_MODEL)

    for l in range(depth):
        mods = mods_all[l]
        w_l = w_in[l]
        w_main = jnp.concatenate([w_l[:, :D_INNER + CONV_DIM], w_l[:, D_INNER + CONV_DIM + 2 * HEADS:]],
                                 axis=1).astype(BF16)
        w_dt = jnp.pad(w_l[:, D_INNER + CONV_DIM:D_INNER + CONV_DIM + 2 * HEADS],
                       ((0, 0), (0, DT_PAD - 2 * HEADS))).astype(BF16)
        dt_bias = jnp.pad(ssd_dt_bias[l].reshape(1, 2 * HEADS), ((0, 0), (0, DT_PAD - 2 * HEADS)))
        alog = jnp.pad(ssd_a_log[l].reshape(1, 2 * HEADS), ((0, 0), (0, DT_PAD - 2 * HEADS)))

        p, dt = _in_proj(h, mods, norm_mix[l].reshape(1, D_MODEL), w_main, w_dt, dt_bias, mod_index)
        xbc = _ssd_conv(p.reshape(b, s_len, P_COLS), ssd_conv_w[l], ssd_conv_b[l].reshape(1, CONV_DIM), n_ctx)
        dt3 = dt.reshape(b, s_len, DT_PAD)
        a2, arow, ea, dw, eal = _decay_tables(dt3, alog)

        cols = jnp.stack([_group_major(v, b, s_len) for v in (a2, ea, dw)], axis=3)
        colslab = cols.transpose(0, 4, 1, 2, 3, 5).reshape(b, GROUPS, s_len, 48)
        rowslab = _group_major(arow, b, s_len).reshape(b, nc, CHUNK, 2, GROUPS, HEADS_PER_GROUP)
        rowslab = rowslab.transpose(0, 4, 3, 1, 5, 2)
        ealx = eal[:, :, 0, :2 * HEADS].reshape(b, nc, 2, GROUPS, HEADS_PER_GROUP).transpose(0, 3, 1, 2, 4)
        ealx = jnp.repeat(ealx, HEAD_DIM, axis=-1)

        y = _ssd_scan(xbc, colslab, rowslab, ealx, spread_mat, n_ctx)
        last = l == depth - 1
        h = _mixer_out(h, y.reshape(t, D_INNER), xbc.reshape(t, CONV_DIM), p, mods, mod_index,
                       jnp.repeat(ssd_d[l], HEAD_DIM).reshape(1, D_INNER), ssd_norm[l].reshape(1, D_INNER),
                       w_ssd_out[l].astype(BF16), sc_conv_w[l], w_sc_out[l].astype(BF16), w_o[l].astype(BF16),
                       tiles_per_batch, n_ctx, latent_only=last)
        ffn_index = (lambda i: (i // (seq // ROW_TILE), 1, 0, 0)) if last else mod_index
        t_ffn = h.shape[0]
        u2, idx, wts = _router(h, mods, ffn_index, norm_ffn[l].reshape(1, D_MODEL), wr_t, br)
        tile_lo, tile_hi, n_used, src_idx, wts_sorted, pos = _routing_tables(idx, wts, t_ffn)
        y_sorted = _experts(tile_lo, tile_hi, n_used, src_idx, u2, wts_sorted,
                            w_gate[l].astype(BF16), w_up[l].astype(BF16), w_down[l].astype(BF16))
        h = _combine(pos, y_sorted, h, mods, ffn_index, final_norm.reshape(1, D_MODEL), final_norm=last)

    return h.reshape(b, seq, D_MODEL)
```
